```python
import jax, jax.numpy as jnp
from jax import lax
import numpy as np

D_MODEL = 1024
BATCH = 2
SEQ = 8192
DEPTH = 1
DEC_BATCH = 128
DEC_SEQ = 4
PAST_LEN = 2048
PAGE_SIZE = 128

HEAD_DIM = 64
N_HEADS = D_MODEL // HEAD_DIM
N_HEADS_A = N_HEADS // 2
N_HEADS_B = N_HEADS - N_HEADS_A
WIDTH_A = N_HEADS_A * HEAD_DIM
WIDTH_B = N_HEADS_B * HEAD_DIM
MOBA_BLOCK = 256
MOBA_TOPK = 3
Q_BLOCK = 128
ROPE_THETA = 500000.0
ROT_DIM = HEAD_DIM // 4
D_FF = -(-8 * D_MODEL // (3 * 256)) * 256
LN_EPS = 1e-5
ALPHA = (2 * DEPTH) ** 0.25
BETA = (8 * DEPTH) ** -0.25

kernel_name = 'moba_stickbreak_hybrid_step'


def layer_norm(x, g, b):
    xf = x.astype(jnp.float32)
    mu = xf.mean(-1, keepdims=True)
    var = jnp.square(xf - mu).mean(-1, keepdims=True)
    return ((xf - mu) * lax.rsqrt(var + LN_EPS) * g + b).astype(x.dtype)


def head_norm(o, g):
    B, T, H, D = o.shape
    of = o.astype(jnp.float32)
    of = of * lax.rsqrt(jnp.mean(jnp.square(of), -1, keepdims=True) + LN_EPS)
    return (of.reshape(B, T, H * D) * g).astype(o.dtype)


def partial_rotary(x, pos):
    half = ROT_DIM // 2
    inv = ROPE_THETA ** (-jnp.arange(half, dtype=jnp.float32) * 2.0 / ROT_DIM)
    ang = pos.astype(jnp.float32)[:, None] * inv
    cos = jnp.cos(ang)[:, None, :]
    sin = jnp.sin(ang)[:, None, :]
    xf = x.astype(jnp.float32)
    x1 = xf[..., :half]
    x2 = xf[..., half:ROT_DIM]
    out = jnp.concatenate([x1 * cos - x2 * sin, x2 * cos + x1 * sin, xf[..., ROT_DIM:]], axis=-1)
    return out.astype(x.dtype)


def modulation(c, w_mod, b_mod):
    m = jax.nn.silu(c) @ w_mod + b_mod
    return tuple(t[:, None, :] for t in jnp.split(m, 6, axis=-1))


def qkv_heads(h, w_in, pos):
    B, T, _ = h.shape
    qkv = h @ w_in
    cuts = [WIDTH_A, 2 * WIDTH_A, 3 * WIDTH_A, 3 * WIDTH_A + WIDTH_B, 3 * WIDTH_A + 2 * WIDTH_B]
    qa, ka, va, qb, kb, vb = jnp.split(qkv, cuts, axis=-1)
    qa = partial_rotary(qa.reshape(B, T, N_HEADS_A, HEAD_DIM), pos)
    ka = partial_rotary(ka.reshape(B, T, N_HEADS_A, HEAD_DIM), pos)
    va = va.reshape(B, T, N_HEADS_A, HEAD_DIM)
    qb = qb.reshape(B, T, N_HEADS_B, HEAD_DIM)
    kb = kb.reshape(B, T, N_HEADS_B, HEAD_DIM)
    vb = vb.reshape(B, T, N_HEADS_B, HEAD_DIM)
    return qa, ka, va, qb, kb, vb


def to_blocks(t):
    B, T, H, D = t.shape
    nb = -(-T // MOBA_BLOCK)
    t = jnp.pad(t, ((0, 0), (0, nb * MOBA_BLOCK - T), (0, 0), (0, 0)))
    return t.reshape(B, nb, MOBA_BLOCK, H, D).transpose(0, 3, 1, 2, 4)


def block_means(k_blk):
    return k_blk.astype(jnp.float32).mean(3).astype(k_blk.dtype)


def moba_attend(q, q_pos, k_blk, v_blk, k_mean, n_top):
    B, H, Q, D = q.shape
    NB = k_blk.shape[2]
    own = q_pos // MOBA_BLOCK
    gate = jnp.einsum('bhqd,bhnd->bhqn', q, k_mean).astype(jnp.float32)
    fully_past = jnp.arange(NB)[None, :] < own[:, None]
    gate = jnp.where(fully_past, gate, -jnp.inf)
    _, top_idx = lax.top_k(gate, n_top)
    sel = jnp.concatenate([top_idx, jnp.broadcast_to(own[None, None, :, None], (B, H, Q, 1))], axis=-1)
    sel_valid = jnp.concatenate([jnp.arange(n_top)[None, :] < own[:, None],
                                 jnp.ones((Q, 1), dtype=bool)], axis=-1)
    bi = jnp.arange(B)[:, None, None]
    hi = jnp.arange(H)[None, :, None]
    scale = HEAD_DIM ** -0.5
    scores = []
    for j in range(n_top + 1):
        kj = k_blk[bi, hi, sel[..., j]]
        s = jnp.einsum('bhqd,bhqpd->bhqp', q, kj).astype(jnp.float32) * scale
        k_pos = sel[..., j][..., None] * MOBA_BLOCK + jnp.arange(MOBA_BLOCK)
        ok = (k_pos <= q_pos[:, None]) & sel_valid[:, j][:, None]
        scores.append(jnp.where(ok, s, -jnp.inf))
    p = jax.nn.softmax(jnp.concatenate(scores, axis=-1), axis=-1).astype(v_blk.dtype)
    out = None
    for j in range(n_top + 1):
        vj = v_blk[bi, hi, sel[..., j]]
        term = jnp.einsum('bhqp,bhqpd->bhqd', p[..., j * MOBA_BLOCK:(j + 1) * MOBA_BLOCK], vj)
        out = term if out is None else out + term
    return out


def stick_breaking(q, k, v, q_pos, k_pos):
    z = jnp.einsum('bhqd,bhkd->bhqk', q, k).astype(jnp.float32) * (HEAD_DIM ** -0.5)
    m = k_pos[None, :] < q_pos[:, None]
    log_keep = jnp.where(m, jax.nn.log_sigmoid(-z), 0.0)
    after = lax.cumsum(log_keep, axis=3, reverse=True) - log_keep
    w = jnp.where(m, jnp.exp(jax.nn.log_sigmoid(z) + after), 0.0)
    return jnp.einsum('bhqk,bhkd->bhqd', w.astype(v.dtype), v)


def prompt_mixers(qa, ka, va, qb, kb, vb):
    B, S = qa.shape[0], qa.shape[1]
    nq = S // Q_BLOCK
    pos = jnp.arange(S, dtype=jnp.int32)
    pos_c = pos.reshape(nq, Q_BLOCK)
    k_blk = to_blocks(ka)
    v_blk = to_blocks(va)
    k_mean = block_means(k_blk)
    n_top = min(MOBA_TOPK, k_blk.shape[2])
    kb_t = kb.transpose(0, 2, 1, 3)
    vb_t = vb.transpose(0, 2, 1, 3)

    def chunks(t):
        return t.reshape(B, nq, Q_BLOCK, t.shape[2], HEAD_DIM).transpose(1, 0, 3, 2, 4)

    def unchunks(o):
        return o.transpose(1, 0, 3, 2, 4).reshape(B, S, o.shape[2], HEAD_DIM)

    o_a = lax.map(lambda a: moba_attend(a[0], a[1], k_blk, v_blk, k_mean, n_top), (chunks(qa), pos_c))
    o_b = lax.map(lambda a: stick_breaking(a[0], kb_t, vb_t, a[1], pos), (chunks(qb), pos_c))
    return unchunks(o_a), unchunks(o_b)


def sample_mixers(qa, ka, va, qb, kb, vb, ck_a, cv_a, ck_b, cv_b, page_table):
    DB, T = qa.shape[0], qa.shape[1]
    past = page_table.shape[1] * PAGE_SIZE

    def with_past(cache, new):
        rows = cache[page_table].reshape(DB, past, cache.shape[2], HEAD_DIM)
        return jnp.concatenate([rows, new], axis=1)

    q_pos = past + jnp.arange(T, dtype=jnp.int32)
    k_pos = jnp.arange(past + T, dtype=jnp.int32)
    k_blk = to_blocks(with_past(ck_a, ka))
    v_blk = to_blocks(with_past(cv_a, va))
    n_top = min(MOBA_TOPK, k_blk.shape[2])
    o_a = moba_attend(qa.transpose(0, 2, 1, 3), q_pos, k_blk, v_blk, block_means(k_blk), n_top)
    o_b = stick_breaking(qb.transpose(0, 2, 1, 3), with_past(ck_b, kb).transpose(0, 2, 1, 3),
                         with_past(cv_b, vb).transpose(0, 2, 1, 3), q_pos, k_pos)
    return o_a.transpose(0, 2, 1, 3), o_b.transpose(0, 2, 1, 3)


def post_mix(x, o_a, o_b, gate_a, shift_f, scale_f, gate_f, g_out_a, g_out_b, w_out,
             ln1_g, ln1_b, w_gate, w_up, w_down, ln2_g, ln2_b):
    mixed = jnp.concatenate([head_norm(o_a, g_out_a), head_norm(o_b, g_out_b)], axis=-1)
    x = layer_norm(ALPHA * x + gate_a * (mixed @ w_out), ln1_g, ln1_b)
    h = x * (1.0 + scale_f) + shift_f
    f = (jax.nn.silu(h @ w_gate) * (h @ w_up)) @ w_down
    return layer_norm(ALPHA * x + gate_f * f, ln2_g, ln2_b)


def setup_inputs(seed: int = 0) -> dict:
    key = jax.random.key(seed)
    ks = jax.random.split(key, 32)
    f32 = jnp.float32
    n_pages = PAST_LEN // PAGE_SIZE
    n_used = DEC_BATCH * n_pages
    n_phys = n_used + n_used // 4
    qkv_cols = 3 * WIDTH_A + 3 * WIDTH_B

    def nrm(k, shape, s):
        return jax.random.normal(k, shape, f32) * s

    page_table = jax.random.permutation(ks[0], n_phys)[:n_used].reshape(DEC_BATCH, n_pages).astype(jnp.int32)
    return {
        'x_prompt': nrm(ks[1], (BATCH, SEQ, D_MODEL), 1.0),
        'x_sample': nrm(ks[2], (DEC_BATCH, DEC_SEQ, D_MODEL), 1.0),
        'cache_moba_k': nrm(ks[3], (DEPTH, n_phys, PAGE_SIZE, N_HEADS_A, HEAD_DIM), 1.0),
        'cache_moba_v': nrm(ks[4], (DEPTH, n_phys, PAGE_SIZE, N_HEADS_A, HEAD_DIM), 1.0),
        'cache_sb_k': nrm(ks[5], (DEPTH, n_phys, PAGE_SIZE, N_HEADS_B, HEAD_DIM), 1.0),
        'cache_sb_v': nrm(ks[6], (DEPTH, n_phys, PAGE_SIZE, N_HEADS_B, HEAD_DIM), 1.0),
        'page_table': page_table,
        'c_prompt': nrm(ks[7], (BATCH, D_MODEL), 1.0),
        'c_sample': nrm(ks[8], (DEC_BATCH, D_MODEL), 1.0),
        'ln0_g': 1.0 + nrm(ks[9], (D_MODEL,), 0.02),
        'ln0_b': nrm(ks[10], (D_MODEL,), 0.02),
        'w_mod': nrm(ks[11], (DEPTH, D_MODEL, 6 * D_MODEL), 0.5 * D_MODEL ** -0.5),
        'b_mod': nrm(ks[12], (DEPTH, 6 * D_MODEL), 0.02),
        'w_in': nrm(ks[13], (DEPTH, D_MODEL, qkv_cols), D_MODEL ** -0.5),
        'g_out_a': 1.0 + nrm(ks[14], (DEPTH, WIDTH_A), 0.02),
        'g_out_b': 1.0 + nrm(ks[15], (DEPTH, WIDTH_B), 0.02),
        'w_out': nrm(ks[16], (DEPTH, D_MODEL, D_MODEL), BETA * D_MODEL ** -0.5),
        'ln1_g': 1.0 + nrm(ks[17], (DEPTH, D_MODEL), 0.02),
        'ln1_b': nrm(ks[18], (DEPTH, D_MODEL), 0.02),
        'w_gate': nrm(ks[19], (DEPTH, D_MODEL, D_FF), D_MODEL ** -0.5),
        'w_up': nrm(ks[20], (DEPTH, D_MODEL, D_FF), D_MODEL ** -0.5),
        'w_down': nrm(ks[21], (DEPTH, D_FF, D_MODEL), BETA * D_FF ** -0.5),
        'ln2_g': 1.0 + nrm(ks[22], (DEPTH, D_MODEL), 0.02),
        'ln2_b': nrm(ks[23], (DEPTH, D_MODEL), 0.02),
    }


def reference(x_prompt, x_sample, cache_moba_k, cache_moba_v, cache_sb_k, cache_sb_v, page_table,
              c_prompt, c_sample, ln0_g, ln0_b, w_mod, b_mod, w_in, g_out_a, g_out_b, w_out,
              ln1_g, ln1_b, w_gate, w_up, w_down, ln2_g, ln2_b):
    pos_p = jnp.arange(x_prompt.shape[1], dtype=jnp.int32)
    pos_s = page_table.shape[1] * PAGE_SIZE + jnp.arange(x_sample.shape[1], dtype=jnp.int32)
    xp = layer_norm(x_prompt, ln0_g, ln0_b)
    xs = layer_norm(x_sample, ln0_g, ln0_b)
    kap, vap, kbp, vbp = [], [], [], []
    kas, vas, kbs, vbs = [], [], [], []
    for l in range(DEPTH):
        sa, ca, ga, sf, cf, gf = modulation(c_prompt, w_mod[l], b_mod[l])
        qa, ka, va, qb, kb, vb = qkv_heads(xp * (1.0 + ca) + sa, w_in[l], pos_p)
        o_a, o_b = prompt_mixers(qa, ka, va, qb, kb, vb)
        xp = post_mix(xp, o_a, o_b, ga, sf, cf, gf, g_out_a[l], g_out_b[l], w_out[l],
                      ln1_g[l], ln1_b[l], w_gate[l], w_up[l], w_down[l], ln2_g[l], ln2_b[l])
        kap.append(ka); vap.append(va); kbp.append(kb); vbp.append(vb)
        sa, ca, ga, sf, cf, gf = modulation(c_sample, w_mod[l], b_mod[l])
        qa, ka, va, qb, kb, vb = qkv_heads(xs * (1.0 + ca) + sa, w_in[l], pos_s)
        o_a, o_b = sample_mixers(qa, ka, va, qb, kb, vb, cache_moba_k[l], cache_moba_v[l],
                                 cache_sb_k[l], cache_sb_v[l], page_table)
        xs = post_mix(xs, o_a, o_b, ga, sf, cf, gf, g_out_a[l], g_out_b[l], w_out[l],
                      ln1_g[l], ln1_b[l], w_gate[l], w_up[l], w_down[l], ln2_g[l], ln2_b[l])
        kas.append(ka); vas.append(va); kbs.append(kb); vbs.append(vb)
    return (xp, xs, jnp.stack(kap), jnp.stack(vap), jnp.stack(kbp), jnp.stack(vbp),
            jnp.stack(kas), jnp.stack(vas), jnp.stack(kbs), jnp.stack(vbs))
```

```python
import functools

import jax
import jax.numpy as jnp
from jax import lax
from jax.experimental import pallas as pl
from jax.experimental.pallas import tpu as pltpu

F32 = jnp.float32
BF16 = jnp.bfloat16

HEAD_DIM = 64
MOBA_BLOCK = 256
MOBA_TOPK = 3
PAGE_SIZE = 128
ROPE_THETA = 500000.0
ROT_DIM = HEAD_DIM // 4
LN_EPS = 1e-5
LANES = 128
HEADS_PER_TILE = LANES // HEAD_DIM
NEG = -1e30
VMEM_LIMIT = 56 * 1024 * 1024


def _dot(a, b):
    return jnp.dot(a, b, preferred_element_type=F32)


def _dot_nt(a, b):
    return lax.dot_general(a, b, (((1,), (1,)), ((), ())), preferred_element_type=F32)


def _dot_tn(a, b):
    return lax.dot_general(a, b, (((0,), (0,)), ((), ())), preferred_element_type=F32)


def _split_bf16(x):
    hi = x.astype(BF16)
    lo = (x - hi.astype(F32)).astype(BF16)
    return hi, lo


def _layer_norm(x, g, b):
    mu = jnp.mean(x, axis=-1, keepdims=True)
    xc = x - mu
    var = jnp.mean(xc * xc, axis=-1, keepdims=True)
    return xc * lax.rsqrt(var + LN_EPS) * g + b


def _softplus(z):
    return jnp.maximum(z, 0.0) + jnp.log1p(jnp.exp(-jnp.abs(z)))


def _params(n_grid):
    return pltpu.CompilerParams(dimension_semantics=("arbitrary",) * n_grid, vmem_limit_bytes=VMEM_LIMIT)


def _const_spec(shape):
    zeros = (0,) * len(shape)
    return pl.BlockSpec(shape, lambda *_: zeros, pipeline_mode=pl.Buffered(1))


def _mod_kernel(c_ref, w_ref, b_ref, o_ref):
    c = c_ref[...]
    a = c / (1.0 + jnp.exp(-c))
    o_ref[...] = _dot(a.astype(BF16), w_ref[...].astype(BF16)) + b_ref[...]


def _mod_call(c_all, w_mod, b_mod, tn=1536):
    r, d = c_all.shape
    n = w_mod.shape[1]
    return pl.pallas_call(
        _mod_kernel,
        out_shape=jax.ShapeDtypeStruct((r, n), F32),
        grid=(n // tn,),
        in_specs=[pl.BlockSpec((r, d), lambda j: (0, 0)),
                  pl.BlockSpec((d, tn), lambda j: (0, j)),
                  pl.BlockSpec((1, tn), lambda j: (0, j))],
        out_specs=pl.BlockSpec((r, tn), lambda j: (0, j)),
        compiler_params=_params(1),
        name="mod",
    )(c_all, w_mod, b_mod)


def _pre_kernel(x_ref, g0_ref, b0_ref, mod_ref, rot_ref, w_ref, qa_ref, qb_ref, ka_ref, va_ref, kb_ref, vb_ref,
                *extra_refs, wa, wb):
    xn = _layer_norm(x_ref[...], g0_ref[...], b0_ref[...])
    h = xn * (1.0 + mod_ref[1]) + mod_ref[0]
    qkv = _dot(h.astype(BF16), w_ref[...])
    cos, s_up, s_dn = rot_ref[0], rot_ref[1], rot_ref[2]

    def rotate(t):
        return t * cos + pltpu.roll(t, LANES - ROT_DIM // 2, 1) * s_up + pltpu.roll(t, ROT_DIM // 2, 1) * s_dn

    for t in range(wa // LANES):
        sl = slice(t * LANES, (t + 1) * LANES)
        qa = rotate(qkv[:, sl])
        ka = rotate(qkv[:, wa + t * LANES:wa + (t + 1) * LANES])
        qa_ref[:, sl] = (qa * HEAD_DIM ** -0.5).astype(BF16)
        ka_ref[:, sl] = ka
        if extra_refs:
            kab_ref, km_ref = extra_refs[0], extra_refs[4]
            kab_ref[:, sl] = ka.astype(BF16)
            for g in range(km_ref.shape[0]):
                blk = ka[g * MOBA_BLOCK:(g + 1) * MOBA_BLOCK, :]
                km_ref[g, :, sl] = jnp.sum(blk, axis=0, keepdims=True) * (1.0 / MOBA_BLOCK)
    o = 3 * wa
    va, kb, vb = qkv[:, 2 * wa:o], qkv[:, o + wb:o + 2 * wb], qkv[:, o + 2 * wb:o + 3 * wb]
    qb_ref[...] = (qkv[:, o:o + wb] * HEAD_DIM ** -0.5).astype(BF16)
    va_ref[...] = va
    kb_ref[...] = kb
    vb_ref[...] = vb
    if extra_refs:
        extra_refs[1][...] = va.astype(BF16)
        extra_refs[2][...] = kb.astype(BF16)
        extra_refs[3][...] = vb.astype(BF16)


def _pre_call(x2, g0, b0, mods, mod_spec, rot, w_in_bf, wa, wb, tm, prompt):
    r, d = x2.shape
    n_rot = rot.shape[1] // tm
    row = lambda w: pl.BlockSpec((tm, w), lambda i: (i, 0))
    outs = [jax.ShapeDtypeStruct((r, wa), BF16), jax.ShapeDtypeStruct((r, wb), BF16),
            jax.ShapeDtypeStruct((r, wa), F32), jax.ShapeDtypeStruct((r, wa), F32),
            jax.ShapeDtypeStruct((r, wb), F32), jax.ShapeDtypeStruct((r, wb), F32)]
    out_specs = [row(wa), row(wb), row(wa), row(wa), row(wb), row(wb)]
    if prompt:
        outs += [jax.ShapeDtypeStruct((r, wa), BF16), jax.ShapeDtypeStruct((r, wa), BF16),
                 jax.ShapeDtypeStruct((r, wb), BF16), jax.ShapeDtypeStruct((r, wb), BF16),
                 jax.ShapeDtypeStruct((r // MOBA_BLOCK, 1, wa), F32)]
        out_specs += [row(wa), row(wa), row(wb), row(wb),
                      pl.BlockSpec((tm // MOBA_BLOCK, 1, wa), lambda i: (i, 0, 0))]
    return pl.pallas_call(
        functools.partial(_pre_kernel, wa=wa, wb=wb),
        out_shape=outs,
        grid=(r // tm,),
        in_specs=[row(d), _const_spec((1, d)), _const_spec((1, d)), mod_spec,
                  pl.BlockSpec((3, tm, LANES), lambda i: (0, i % n_rot, 0)),
                  _const_spec(w_in_bf.shape)],
        out_specs=out_specs,
        compiler_params=_params(1),
        name="pre",
    )(x2, g0, b0, mods, rot, w_in_bf)


def _head_norm_pair(o, hmask):
    oh = jnp.where(hmask, o, 0.0)
    ss = jnp.sum(oh * oh, axis=1, keepdims=True) * (1.0 / HEAD_DIM)
    return oh * lax.rsqrt(ss + LN_EPS)


def _moba_kernel(q_ref, k_ref, v_ref, km_ref, go_ref, o_ref, g_ref, acc_ref, m_ref, l_ref, *, tq, nblk):
    i = pl.program_id(2)
    q = q_ref[...]
    lane = lax.broadcasted_iota(jnp.int32, (1, LANES), 1)
    qpos_row = i * tq + lax.broadcasted_iota(jnp.int32, (1, tq), 1)
    own_row = qpos_row // MOBA_BLOCK
    qpos_col = i * tq + lax.broadcasted_iota(jnp.int32, (tq, 1), 0)
    kofs = lax.broadcasted_iota(jnp.int32, (1, MOBA_BLOCK), 1)
    n_idx = lax.broadcasted_iota(jnp.int32, (nblk, tq), 0)
    n_past = (i * tq) // MOBA_BLOCK
    n_vis = ((i + 1) * tq) // MOBA_BLOCK
    km_hi, km_lo = _split_bf16(km_ref[0])
    valid = n_idx < own_row
    outs = []
    for hh in range(HEADS_PER_TILE):
        hmask = (lane // HEAD_DIM) == hh
        qh = jnp.where(hmask, q, jnp.zeros_like(q))
        g = jnp.where(valid, _dot_nt(km_hi, qh) + _dot_nt(km_lo, qh), -jnp.inf)
        g_ref[...] = g

        def rank_step(n2, rank):
            row = g_ref[pl.ds(n2, 1), :]
            beats = (row > g) | ((row == g) & (n2 < n_idx))
            return rank + beats.astype(jnp.int32)

        rank = lax.fori_loop(0, n_vis, rank_step, jnp.zeros((nblk, tq), jnp.int32))
        sel = (valid & (rank < MOBA_TOPK)) | (n_idx == own_row)
        bias_t = jnp.where(sel, 0.0, NEG).astype(BF16)
        base = HEAD_DIM * (1 - hh)
        place = (lax.broadcasted_iota(jnp.int32, (nblk, LANES), 1)
                 == lax.broadcasted_iota(jnp.int32, (nblk, LANES), 0) + base).astype(BF16)
        q_aug = jnp.where(hmask, q, _dot_tn(bias_t, place).astype(BF16))

        acc_ref[...] = jnp.zeros_like(acc_ref)
        m_ref[...] = jnp.full_like(m_ref, NEG)
        l_ref[...] = jnp.zeros_like(l_ref)

        def step(j, causal):
            start = pl.multiple_of(j * MOBA_BLOCK, MOBA_BLOCK)
            ks = k_ref[pl.ds(start, MOBA_BLOCK), :]
            vs = v_ref[pl.ds(start, MOBA_BLOCK), :]
            k_aug = jnp.where(hmask, ks, ((lane - base) == j).astype(BF16))
            s = _dot_nt(q_aug, k_aug)
            if causal:
                s = jnp.where(j * MOBA_BLOCK + kofs <= qpos_col, s, NEG)
            m_prev = m_ref[...]
            m_new = jnp.maximum(m_prev, jnp.max(s, axis=1, keepdims=True))
            alpha = jnp.exp(m_prev - m_new)
            p = jnp.exp(s - jnp.concatenate([m_new] * (MOBA_BLOCK // LANES), axis=1))
            l_ref[...] = alpha * l_ref[...] + jnp.sum(p, axis=1, keepdims=True)
            acc_ref[...] = alpha * acc_ref[...] + _dot(p.astype(BF16), vs)
            m_ref[...] = m_new

        def past_step(j, carry):
            step(j, False)
            return carry

        lax.fori_loop(0, n_past, past_step, 0)
        for d in range(tq // MOBA_BLOCK):
            step(n_past + d, True)
        outs.append(_head_norm_pair(acc_ref[...] / l_ref[...], hmask))
    o_ref[...] = ((outs[0] + outs[1]) * go_ref[...]).astype(o_ref.dtype)


def _moba_call(qa, kab, vab, kmean, g_out, b, s, tq):
    nq = s // tq
    nblk = s // MOBA_BLOCK
    n_tiles = qa.shape[1] // LANES
    return pl.pallas_call(
        functools.partial(_moba_kernel, tq=tq, nblk=nblk),
        out_shape=jax.ShapeDtypeStruct(qa.shape, BF16),
        grid=(b, n_tiles, nq),
        in_specs=[pl.BlockSpec((tq, LANES), lambda bi, hp, i: (bi * nq + i, hp)),
                  pl.BlockSpec((s, LANES), lambda bi, hp, i: (bi, hp)),
                  pl.BlockSpec((s, LANES), lambda bi, hp, i: (bi, hp)),
                  pl.BlockSpec((1, nblk, LANES), lambda bi, hp, i: (bi, 0, hp)),
                  pl.BlockSpec((1, LANES), lambda bi, hp, i: (0, hp))],
        out_specs=pl.BlockSpec((tq, LANES), lambda bi, hp, i: (bi * nq + i, hp)),
        scratch_shapes=[pltpu.VMEM((nblk, tq), F32), pltpu.VMEM((tq, LANES), F32),
                        pltpu.VMEM((tq, LANES), F32), pltpu.VMEM((tq, LANES), F32)],
        compiler_params=_params(3),
        name="moba",
    )(qa, kab, vab, kmean, g_out)


def _sb_kernel(q_ref, k_ref, v_ref, go_ref, o_ref, acc_ref, car_ref, *, tq):
    i = pl.program_id(2)
    q = q_ref[...]
    lane = lax.broadcasted_iota(jnp.int32, (1, LANES), 1)
    qpos_col = i * tq + lax.broadcasted_iota(jnp.int32, (tq, 1), 0)
    kofs = lax.broadcasted_iota(jnp.int32, (1, MOBA_BLOCK), 1)
    later = (lax.broadcasted_iota(jnp.int32, (MOBA_BLOCK, MOBA_BLOCK), 0)
             > lax.broadcasted_iota(jnp.int32, (MOBA_BLOCK, MOBA_BLOCK), 1)).astype(BF16)
    n_past = (i * tq) // MOBA_BLOCK
    outs = []
    for hh in range(HEADS_PER_TILE):
        hmask = (lane // HEAD_DIM) == hh
        qh = jnp.where(hmask, q, jnp.zeros_like(q))
        acc_ref[...] = jnp.zeros_like(acc_ref)
        car_ref[...] = jnp.zeros_like(car_ref)

        def step(j, causal):
            start = pl.multiple_of(j * MOBA_BLOCK, MOBA_BLOCK)
            ks = k_ref[pl.ds(start, MOBA_BLOCK), :]
            vs = v_ref[pl.ds(start, MOBA_BLOCK), :]
            z = _dot_nt(qh, ks)
            sp = _softplus(z)
            lk = -sp
            if causal:
                vis = j * MOBA_BLOCK + kofs < qpos_col
                lk = jnp.where(vis, lk, 0.0)
            lk_hi, lk_lo = _split_bf16(lk)
            car = car_ref[...]
            after = _dot(lk_hi, later) + _dot(lk_lo, later) + jnp.concatenate([car] * (MOBA_BLOCK // LANES), axis=1)
            w = jnp.exp(z - sp + after)
            if causal:
                w = jnp.where(vis, w, 0.0)
            acc_ref[...] += _dot(w.astype(BF16), vs)
            car_ref[...] = car + jnp.sum(lk, axis=1, keepdims=True)

        for d in range(tq // MOBA_BLOCK):
            step(n_past + tq // MOBA_BLOCK - 1 - d, True)

        def past_step(t, carry):
            step(n_past - 1 - t, False)
            return carry

        lax.fori_loop(0, n_past, past_step, 0)
        outs.append(_head_norm_pair(acc_ref[...], hmask))
    o_ref[...] = ((outs[0] + outs[1]) * go_ref[...]).astype(o_ref.dtype)


def _sb_call(qb, kbb, vbb, g_out, b, s, tq):
    nq = s // tq
    n_tiles = qb.shape[1] // LANES
    return pl.pallas_call(
        functools.partial(_sb_kernel, tq=tq),
        out_shape=jax.ShapeDtypeStruct(qb.shape, BF16),
        grid=(b, n_tiles, nq),
        in_specs=[pl.BlockSpec((tq, LANES), lambda bi, hp, i: (bi * nq + i, hp)),
                  pl.BlockSpec((s, LANES), lambda bi, hp, i: (bi, hp)),
                  pl.BlockSpec((s, LANES), lambda bi, hp, i: (bi, hp)),
                  pl.BlockSpec((1, LANES), lambda bi, hp, i: (0, hp))],
        out_specs=pl.BlockSpec((tq, LANES), lambda bi, hp, i: (bi * nq + i, hp)),
        scratch_shapes=[pltpu.VMEM((tq, LANES), F32), pltpu.VMEM((tq, LANES), F32)],
        compiler_params=_params(3),
        name="sb",
    )(qb, kbb, vbb, g_out)


def _block_diag_queries(q, n_heads):
    t, w = q.shape
    head_of_lane = lax.broadcasted_iota(jnp.int32, (n_heads, w), 1) // HEAD_DIM
    hmask = head_of_lane == lax.broadcasted_iota(jnp.int32, (n_heads, w), 0)
    rows = [jnp.where(hmask, jnp.broadcast_to(q[ti:ti + 1, :], (n_heads, w)), 0.0) for ti in range(t)]
    return jnp.concatenate(rows, axis=0), jnp.concatenate([hmask] * t, axis=0)


def _collapse_heads(o, hmask_rows, gain, n_tok, n_heads):
    oh = jnp.where(hmask_rows, o, 0.0)
    ss = jnp.sum(oh * oh, axis=1, keepdims=True) * (1.0 / HEAD_DIM)
    of = oh * lax.rsqrt(ss + LN_EPS)
    rows = [jnp.sum(of[ti * n_heads:(ti + 1) * n_heads, :], axis=0, keepdims=True) for ti in range(n_tok)]
    return jnp.concatenate(rows, axis=0) * gain


def _smoba_kernel(pt_ref, q_ref, kn_ref, vn_ref, go_ref, *rest, n_pages, n_tok, n_heads):
    del pt_ref
    k_pages, v_pages, o_ref = rest[:n_pages], rest[n_pages:2 * n_pages], rest[2 * n_pages]
    qbd, hmask_rows = _block_diag_queries(q_ref[0].astype(F32), n_heads)
    qbd_bf = qbd.astype(BF16)
    kn, vn = kn_ref[0], vn_ref[0]
    tok_of_row = lax.broadcasted_iota(jnp.int32, (n_tok * n_heads, 1), 0) // n_heads
    pages_per_block = MOBA_BLOCK // PAGE_SIZE
    n_blocks = n_pages // pages_per_block
    s_pages = [_dot(qbd_bf, k_pages[p][...].astype(BF16)) for p in range(n_pages)]
    gates = [sum(jnp.sum(s_pages[n * pages_per_block + u], axis=1, keepdims=True) for u in range(pages_per_block))
             for n in range(n_blocks)]
    biases = []
    for n in range(n_blocks):
        rank = jnp.zeros_like(gates[n])
        for n2 in range(n_blocks):
            if n2 != n:
                beats = (gates[n2] > gates[n]) | ((gates[n2] == gates[n]) & (n2 < n))
                rank = rank + beats.astype(F32)
        biases.append(jnp.where(rank < min(MOBA_TOPK, n_blocks), 0.0, NEG))
    s_new = [jnp.where(t <= tok_of_row, jnp.sum(qbd * kn[t:t + 1, :], axis=1, keepdims=True), NEG)
             for t in range(n_tok)]
    s_sel = [s_pages[p] + biases[p // pages_per_block] for p in range(n_pages)]
    m = functools.reduce(jnp.maximum, s_new)
    for p in range(n_pages):
        m = jnp.maximum(m, jnp.max(s_sel[p], axis=1, keepdims=True))
    acc = jnp.zeros(qbd.shape, F32)
    l = jnp.zeros_like(m)
    for t in range(n_tok):
        pt = jnp.exp(s_new[t] - m)
        l = l + pt
        acc = acc + pt * vn[t:t + 1, :]
    for p in range(n_pages):
        pp = jnp.exp(s_sel[p] - m)
        l = l + jnp.sum(pp, axis=1, keepdims=True)
        acc = acc + _dot_nt(pp.astype(BF16), v_pages[p][...].astype(BF16))
    o_ref[0] = _collapse_heads(acc / l, hmask_rows, go_ref[...], n_tok, n_heads).astype(o_ref.dtype)


def _ssb_kernel(pt_ref, q_ref, kn_ref, vn_ref, go_ref, *rest, n_pages, n_tok, n_heads):
    del pt_ref
    k_pages, v_pages, o_ref = rest[:n_pages], rest[n_pages:2 * n_pages], rest[2 * n_pages]
    qbd, hmask_rows = _block_diag_queries(q_ref[0].astype(F32), n_heads)
    qbd_bf = qbd.astype(BF16)
    kn, vn = kn_ref[0], vn_ref[0]
    tok_of_row = lax.broadcasted_iota(jnp.int32, (n_tok * n_heads, 1), 0) // n_heads
    later = (lax.broadcasted_iota(jnp.int32, (PAGE_SIZE, PAGE_SIZE), 0)
             > lax.broadcasted_iota(jnp.int32, (PAGE_SIZE, PAGE_SIZE), 1)).astype(BF16)
    acc = jnp.zeros(qbd.shape, F32)
    carry = jnp.zeros((n_tok * n_heads, 1), F32)
    for t in reversed(range(n_tok)):
        z = jnp.sum(qbd * kn[t:t + 1, :], axis=1, keepdims=True)
        sp = _softplus(z)
        vis = t < tok_of_row
        acc = acc + jnp.where(vis, jnp.exp(z - sp + carry), 0.0) * vn[t:t + 1, :]
        carry = carry + jnp.where(vis, -sp, 0.0)
    for p in reversed(range(n_pages)):
        z = _dot(qbd_bf, k_pages[p][...].astype(BF16))
        sp = _softplus(z)
        lk = -sp
        lk_hi, lk_lo = _split_bf16(lk)
        after = _dot(lk_hi, later) + _dot(lk_lo, later) + carry
        acc = acc + _dot_nt(jnp.exp(z - sp + after).astype(BF16), v_pages[p][...].astype(BF16))
        carry = carry + jnp.sum(lk, axis=1, keepdims=True)
    o_ref[0] = _collapse_heads(acc, hmask_rows, go_ref[...], n_tok, n_heads).astype(o_ref.dtype)


def _sample_attn_call(body, name, page_table, q, k_new, v_new, g_out, cache_k, cache_v):
    db, n_pages = page_table.shape
    _, t, w = q.shape
    n_heads = w // HEAD_DIM
    tok = lambda: pl.BlockSpec((1, t, w), lambda bi, pt: (bi, 0, 0))
    page = lambda p: pl.BlockSpec((None, w, PAGE_SIZE), lambda bi, pt: (pt[bi, p], 0, 0))
    grid_spec = pltpu.PrefetchScalarGridSpec(
        num_scalar_prefetch=1,
        grid=(db,),
        in_specs=[tok(), tok(), tok(), pl.BlockSpec((1, w), lambda bi, pt: (0, 0))]
                 + [page(p) for p in range(n_pages)] * 2,
        out_specs=tok(),
    )
    return pl.pallas_call(
        functools.partial(body, n_pages=n_pages, n_tok=t, n_heads=n_heads),
        out_shape=jax.ShapeDtypeStruct(q.shape, BF16),
        grid_spec=grid_spec,
        compiler_params=_params(1),
        name=name,
    )(page_table, q, k_new, v_new, g_out, *([cache_k] * n_pages), *([cache_v] * n_pages))


def _post_kernel(x_ref, g0_ref, b0_ref, ma_ref, mb_ref, mod_ref, wo_ref, g1_ref, b1_ref,
                 wg_ref, wu_ref, wd_ref, g2_ref, b2_ref, o_ref, *, alpha, wa, ff_chunk):
    xn = _layer_norm(x_ref[...], g0_ref[...], b0_ref[...])
    attn = _dot(ma_ref[...], wo_ref[:wa, :]) + _dot(mb_ref[...], wo_ref[wa:, :])
    x1 = _layer_norm(alpha * xn + mod_ref[2] * attn, g1_ref[...], b1_ref[...])
    hb = (x1 * (1.0 + mod_ref[4]) + mod_ref[3]).astype(BF16)
    f = jnp.zeros(x1.shape, F32)
    for c in range(0, wg_ref.shape[1], ff_chunk):
        gate = _dot(hb, wg_ref[:, c:c + ff_chunk])
        up = _dot(hb, wu_ref[:, c:c + ff_chunk])
        act = gate / (1.0 + jnp.exp(-gate)) * up
        f = f + _dot(act.astype(BF16), wd_ref[c:c + ff_chunk, :])
    o_ref[...] = _layer_norm(alpha * x1 + mod_ref[5] * f, g2_ref[...], b2_ref[...])


def _post_call(x2, g0, b0, ma, mb, mods, mod_spec, wo, g1, b1, wg, wu, wd, g2, b2, alpha, tm, ff_chunk):
    r, d = x2.shape
    wa = ma.shape[1]
    row = lambda w: pl.BlockSpec((tm, w), lambda i: (i, 0))
    vec = lambda: _const_spec((1, d))
    return pl.pallas_call(
        functools.partial(_post_kernel, alpha=alpha, wa=wa, ff_chunk=ff_chunk),
        out_shape=jax.ShapeDtypeStruct((r, d), F32),
        grid=(r // tm,),
        in_specs=[row(d), vec(), vec(), row(wa), row(mb.shape[1]), mod_spec, _const_spec(wo.shape), vec(), vec(),
                  _const_spec(wg.shape), _const_spec(wu.shape), _const_spec(wd.shape), vec(), vec()],
        out_specs=row(d),
        compiler_params=_params(1),
        name="post",
    )(x2, g0, b0, ma, mb, mods, wo, g1, b1, wg, wu, wd, g2, b2)


def _rotary_tables(pos):
    half = ROT_DIM // 2
    inv = ROPE_THETA ** (-jnp.arange(half, dtype=F32) * 2.0 / ROT_DIM)
    ang = pos.astype(F32)[:, None] * inv
    cos, sin = jnp.cos(ang), jnp.sin(ang)
    n = pos.shape[0]
    rest = HEAD_DIM - ROT_DIM
    c = jnp.concatenate([cos, cos, jnp.ones((n, rest), F32)], axis=1)
    s_up = jnp.concatenate([-sin, jnp.zeros((n, half + rest), F32)], axis=1)
    s_dn = jnp.concatenate([jnp.zeros((n, half), F32), sin, jnp.zeros((n, rest), F32)], axis=1)
    return jnp.stack([jnp.tile(t, (1, HEADS_PER_TILE)) for t in (c, s_up, s_dn)])


def kernel(x_prompt, x_sample, cache_moba_k, cache_moba_v, cache_sb_k, cache_sb_v, page_table, c_prompt, c_sample,
           ln0_g, ln0_b, w_mod, b_mod, w_in, g_out_a, g_out_b, w_out, ln1_g, ln1_b, w_gate, w_up, w_down,
           ln2_g, ln2_b):
    b, s, d = x_prompt.shape
    db, t, _ = x_sample.shape
    depth = w_mod.shape[0]
    assert depth == 1, "ln0 is fused into the first layer's kernels; only a one-layer trunk is supported"
    wa, wb = g_out_a.shape[1], g_out_b.shape[1]
    ha, hb = wa // HEAD_DIM, wb // HEAD_DIM
    n_pages = page_table.shape[1]
    past = n_pages * PAGE_SIZE
    alpha = (2 * depth) ** 0.25
    vec = lambda a: a.reshape(1, -1)

    n_mod = 6
    rows = b + db
    pad = -rows % 8
    c_all = jnp.concatenate([c_prompt, c_sample, jnp.zeros((pad, d), F32)], axis=0)
    m = _mod_call(c_all, w_mod[0], b_mod[0].reshape(1, -1))
    mods_p = m[:b].reshape(b, n_mod, 1, d)
    mods_s = jnp.repeat(m[b:rows].reshape(db, n_mod, d), t, axis=0).transpose(1, 0, 2)

    w_in_bf = w_in[0].astype(BF16)
    wo_bf, wg_bf, wu_bf, wd_bf = (w[0].astype(BF16) for w in (w_out, w_gate, w_up, w_down))
    g0, b0 = vec(ln0_g), vec(ln0_b)
    ln = [vec(a[0]) for a in (ln1_g, ln1_b, ln2_g, ln2_b)]
    goa, gob = g_out_a[0].reshape(1, wa), g_out_b[0].reshape(1, wb)

    tm = 512
    tq = 512
    ff_chunk = 1408

    xp2 = x_prompt.reshape(b * s, d)
    mod_spec_p = pl.BlockSpec((None, n_mod, 1, d), lambda i: (i // (s // tm), 0, 0, 0))
    rot_p = _rotary_tables(jnp.arange(s, dtype=jnp.int32))
    (qa, qb, ka, va, kb, vb, kab, vab, kbb, vbb, kmean) = _pre_call(
        xp2, g0, b0, mods_p, mod_spec_p, rot_p, w_in_bf, wa, wb, tm, True)
    mixed_a = _moba_call(qa, kab, vab, kmean.reshape(b, s // MOBA_BLOCK, wa), goa, b, s, tq)
    mixed_b = _sb_call(qb, kbb, vbb, gob, b, s, tq)
    y_p = _post_call(xp2, g0, b0, mixed_a, mixed_b, mods_p, mod_spec_p, wo_bf, ln[0], ln[1],
                     wg_bf, wu_bf, wd_bf, ln[2], ln[3], alpha, tm, ff_chunk)

    rs = db * t
    xs2 = x_sample.reshape(rs, d)
    mod_spec_s = pl.BlockSpec((n_mod, rs, d), lambda i: (0, 0, 0))
    rot_s = _rotary_tables(jnp.tile(past + jnp.arange(t, dtype=jnp.int32), db))
    (sqa, sqb, ska, sva, skb, svb) = _pre_call(
        xs2, g0, b0, mods_s, mod_spec_s, rot_s, w_in_bf, wa, wb, rs, False)
    tok3 = lambda a: a.reshape(db, t, a.shape[-1])
    pages = lambda c: c[0].transpose(0, 2, 3, 1).reshape(c.shape[1], -1, PAGE_SIZE)
    smixed_a = _sample_attn_call(_smoba_kernel, "smoba", page_table, tok3(sqa), tok3(ska), tok3(sva), goa,
                                 pages(cache_moba_k), pages(cache_moba_v))
    smixed_b = _sample_attn_call(_ssb_kernel, "ssb", page_table, tok3(sqb), tok3(skb), tok3(svb), gob,
                                 pages(cache_sb_k), pages(cache_sb_v))
    y_s = _post_call(xs2, g0, b0, smixed_a.reshape(rs, wa), smixed_b.reshape(rs, wb), mods_s, mod_spec_s,
                     wo_bf, ln[0], ln[1], wg_bf, wu_bf, wd_bf, ln[2], ln[3], alpha, rs, ff_chunk)

    return (y_p.reshape(b, s, d), y_s.reshape(db, t, d),
            ka.reshape(1, b, s, ha, HEAD_DIM), va.reshape(1, b, s, ha, HEAD_DIM),
            kb.reshape(1, b, s, hb, HEAD_DIM), vb.reshape(1, b, s, hb, HEAD_DIM),
            ska.reshape(1, db, t, ha, HEAD_DIM), sva.reshape(1, db, t, ha, HEAD_DIM),
            skb.reshape(1, db, t, hb, HEAD_DIM), svb.reshape(1, db, t, hb, HEAD_DIM))
```

```python
import functools

import jax
import jax.numpy as jnp
from jax import lax
from jax.experimental import pallas as pl
from jax.experimental.pallas import tpu as pltpu

F32 = jnp.float32
BF16 = jnp.bfloat16

HEAD_DIM = 64
MOBA_BLOCK = 256
MOBA_TOPK = 3
PAGE_SIZE = 128
ROPE_THETA = 500000.0
ROT_DIM = HEAD_DIM // 4
LN_EPS = 1e-5
LANES = 128
HEADS_PER_TILE = LANES // HEAD_DIM
NEG = -1e30
UNDERFLOW_LOG = -110.0
VMEM_LIMIT = 56 * 1024 * 1024


def _dot(a, b):
    return jnp.dot(a, b, preferred_element_type=F32)


def _dot_nt(a, b):
    return lax.dot_general(a, b, (((1,), (1,)), ((), ())), preferred_element_type=F32)


def _dot_tn(a, b):
    return lax.dot_general(a, b, (((0,), (0,)), ((), ())), preferred_element_type=F32)


def _split_bf16(x):
    hi = x.astype(BF16)
    lo = (x - hi.astype(F32)).astype(BF16)
    return hi, lo


def _layer_norm(x, g, b):
    mu = jnp.mean(x, axis=-1, keepdims=True)
    xc = x - mu
    var = jnp.mean(xc * xc, axis=-1, keepdims=True)
    return xc * lax.rsqrt(var + LN_EPS) * g + b


def _softplus(z):
    return jnp.maximum(z, 0.0) + jnp.log(1.0 + jnp.exp(-jnp.abs(z)))


def _later_matrix(n):
    return (lax.broadcasted_iota(jnp.int32, (n, n), 0) > lax.broadcasted_iota(jnp.int32, (n, n), 1)).astype(BF16)


def _params(n_grid):
    return pltpu.CompilerParams(dimension_semantics=("arbitrary",) * n_grid, vmem_limit_bytes=VMEM_LIMIT)


def _const_spec(shape):
    zeros = (0,) * len(shape)
    return pl.BlockSpec(shape, lambda *_: zeros, pipeline_mode=pl.Buffered(1))


def _mod_kernel(c_ref, w_ref, b_ref, o_ref):
    c = c_ref[...]
    a = c / (1.0 + jnp.exp(-c))
    o_ref[...] = _dot(a.astype(BF16), w_ref[...].astype(BF16)) + b_ref[...]


def _mod_call(c_all, w_mod, b_mod, tn=1536):
    r, d = c_all.shape
    n = w_mod.shape[1]
    return pl.pallas_call(
        _mod_kernel,
        out_shape=jax.ShapeDtypeStruct((r, n), F32),
        grid=(n // tn,),
        in_specs=[pl.BlockSpec((r, d), lambda j: (0, 0)),
                  pl.BlockSpec((d, tn), lambda j: (0, j)),
                  pl.BlockSpec((1, tn), lambda j: (0, j))],
        out_specs=pl.BlockSpec((r, tn), lambda j: (0, j)),
        compiler_params=_params(1),
        name="mod",
    )(c_all, w_mod, b_mod)


def _pre_kernel(x_ref, g0_ref, b0_ref, mod_ref, rot_ref, w_ref, qa_ref, qb_ref, ka_ref, va_ref, kb_ref, vb_ref,
                *extra_refs, wa, wb):
    xn = _layer_norm(x_ref[...], g0_ref[...], b0_ref[...])
    h = xn * (1.0 + mod_ref[1]) + mod_ref[0]
    qkv = _dot(h.astype(BF16), w_ref[...])
    cos, s_up, s_dn = rot_ref[0], rot_ref[1], rot_ref[2]

    def rotate(t):
        return t * cos + pltpu.roll(t, LANES - ROT_DIM // 2, 1) * s_up + pltpu.roll(t, ROT_DIM // 2, 1) * s_dn

    for t in range(wa // LANES):
        sl = slice(t * LANES, (t + 1) * LANES)
        qa = rotate(qkv[:, sl])
        ka = rotate(qkv[:, wa + t * LANES:wa + (t + 1) * LANES])
        qa_ref[:, sl] = (qa * HEAD_DIM ** -0.5).astype(BF16)
        ka_ref[:, sl] = ka
        if extra_refs:
            kab_ref, km_ref = extra_refs[0], extra_refs[4]
            kab_ref[:, sl] = ka.astype(BF16)
            for g in range(km_ref.shape[0]):
                blk = ka[g * MOBA_BLOCK:(g + 1) * MOBA_BLOCK, :]
                km_ref[g, :, sl] = jnp.sum(blk, axis=0, keepdims=True) * (1.0 / MOBA_BLOCK)
    o = 3 * wa
    va, kb, vb = qkv[:, 2 * wa:o], qkv[:, o + wb:o + 2 * wb], qkv[:, o + 2 * wb:o + 3 * wb]
    qb_ref[...] = (qkv[:, o:o + wb] * HEAD_DIM ** -0.5).astype(BF16)
    va_ref[...] = va
    kb_ref[...] = kb
    vb_ref[...] = vb
    if extra_refs:
        extra_refs[1][...] = va.astype(BF16)
        extra_refs[2][...] = kb.astype(BF16)
        extra_refs[3][...] = vb.astype(BF16)


def _pre_call(x2, g0, b0, mods, mod_spec, rot, w_in_bf, wa, wb, tm, prompt):
    r, d = x2.shape
    n_rot = rot.shape[1] // tm
    row = lambda w: pl.BlockSpec((tm, w), lambda i: (i, 0))
    outs = [jax.ShapeDtypeStruct((r, wa), BF16), jax.ShapeDtypeStruct((r, wb), BF16),
            jax.ShapeDtypeStruct((r, wa), F32), jax.ShapeDtypeStruct((r, wa), F32),
            jax.ShapeDtypeStruct((r, wb), F32), jax.ShapeDtypeStruct((r, wb), F32)]
    out_specs = [row(wa), row(wb), row(wa), row(wa), row(wb), row(wb)]
    if prompt:
        outs += [jax.ShapeDtypeStruct((r, wa), BF16), jax.ShapeDtypeStruct((r, wa), BF16),
                 jax.ShapeDtypeStruct((r, wb), BF16), jax.ShapeDtypeStruct((r, wb), BF16),
                 jax.ShapeDtypeStruct((r // MOBA_BLOCK, 1, wa), F32)]
        out_specs += [row(wa), row(wa), row(wb), row(wb),
                      pl.BlockSpec((tm // MOBA_BLOCK, 1, wa), lambda i: (i, 0, 0))]
    return pl.pallas_call(
        functools.partial(_pre_kernel, wa=wa, wb=wb),
        out_shape=outs,
        grid=(r // tm,),
        in_specs=[row(d), _const_spec((1, d)), _const_spec((1, d)), mod_spec,
                  pl.BlockSpec((3, tm, LANES), lambda i: (0, i % n_rot, 0)),
                  _const_spec(w_in_bf.shape)],
        out_specs=out_specs,
        compiler_params=_params(1),
        name="pre",
    )(x2, g0, b0, mods, rot, w_in_bf)


def _merge_heads_and_norm(o2, gain, tq):
    lane = lax.broadcasted_iota(jnp.int32, (1, LANES), 1)
    first = lane < HEAD_DIM
    o = jnp.where(first, o2[:tq], o2[tq:])
    sq = o * o
    ss0 = jnp.sum(jnp.where(first, sq, 0.0), axis=1, keepdims=True)
    ss1 = jnp.sum(jnp.where(first, 0.0, sq), axis=1, keepdims=True)
    scale = jnp.where(first, lax.rsqrt(ss0 * (1.0 / HEAD_DIM) + LN_EPS), lax.rsqrt(ss1 * (1.0 / HEAD_DIM) + LN_EPS))
    return o * scale * gain


def _moba_kernel(q_ref, k_ref, v_ref, km_ref, go_ref, o_ref, g_ref, acc_ref, m_ref, l_ref, *, tq, nblk):
    i = pl.program_id(2)
    q = q_ref[...]
    lane = lax.broadcasted_iota(jnp.int32, (1, LANES), 1)
    n_idx = lax.broadcasted_iota(jnp.int32, (nblk, tq), 0)
    valid = n_idx < i
    km_hi, km_lo = _split_bf16(km_ref[0])
    place = (lax.broadcasted_iota(jnp.int32, (nblk, LANES), 1)
             == lax.broadcasted_iota(jnp.int32, (nblk, LANES), 0)).astype(BF16)
    halves = []
    for hh in range(HEADS_PER_TILE):
        qh = jnp.where((lane // HEAD_DIM) == hh, q, jnp.zeros_like(q))
        g = jnp.where(valid, _dot_nt(km_hi, qh) + _dot_nt(km_lo, qh), -jnp.inf)
        g_ref[...] = g

        def rank_step(n2, rank):
            row = g_ref[pl.ds(n2, 1), :]
            beats = (row > g) | ((row == g) & (n2 < n_idx))
            return rank + beats.astype(jnp.int32)

        rank = lax.fori_loop(0, i, rank_step, jnp.zeros((nblk, tq), jnp.int32))
        sel = (valid & (rank < MOBA_TOPK)) | (n_idx == i)
        bias_t = jnp.where(sel, 0.0, NEG).astype(BF16)
        halves.append(jnp.concatenate([qh, _dot_tn(bias_t, place).astype(BF16)], axis=1))
    q2 = jnp.concatenate(halves, axis=0)
    row_in_tile = lax.broadcasted_iota(jnp.int32, (2 * tq, 1), 0) & (tq - 1)
    kofs = lax.broadcasted_iota(jnp.int32, (1, MOBA_BLOCK), 1)

    acc_ref[...] = jnp.zeros_like(acc_ref)
    m_ref[...] = jnp.full_like(m_ref, NEG)
    l_ref[...] = jnp.zeros_like(l_ref)

    def step(j, causal):
        start = pl.multiple_of(j * MOBA_BLOCK, MOBA_BLOCK)
        ks = k_ref[pl.ds(start, MOBA_BLOCK), :]
        vs = v_ref[pl.ds(start, MOBA_BLOCK), :]
        onehot = jnp.broadcast_to((lane == j).astype(BF16), (MOBA_BLOCK, LANES))
        s = _dot_nt(q2, jnp.concatenate([ks, onehot], axis=1))
        if causal:
            s = jnp.where(kofs <= row_in_tile, s, NEG)
        m_prev = m_ref[...]
        m_new = jnp.maximum(m_prev, jnp.max(s, axis=1, keepdims=True))
        alpha = jnp.exp(m_prev - m_new)
        p = jnp.exp(s - jnp.concatenate([m_new] * (MOBA_BLOCK // LANES), axis=1))
        l_ref[...] = alpha * l_ref[...] + jnp.sum(p, axis=1, keepdims=True)
        acc_ref[...] = alpha * acc_ref[...] + _dot(p.astype(BF16), vs)
        m_ref[...] = m_new

    def past_step(j, carry):
        step(j, False)
        return carry

    lax.fori_loop(0, i, past_step, 0)
    step(i, True)
    o_ref[...] = _merge_heads_and_norm(acc_ref[...] / l_ref[...], go_ref[...], tq).astype(o_ref.dtype)


def _moba_call(qa, kab, vab, kmean, g_out, b, s):
    tq = MOBA_BLOCK
    nq = s // tq
    nblk = s // MOBA_BLOCK
    n_tiles = qa.shape[1] // LANES
    return pl.pallas_call(
        functools.partial(_moba_kernel, tq=tq, nblk=nblk),
        out_shape=jax.ShapeDtypeStruct(qa.shape, BF16),
        grid=(b, n_tiles, nq),
        in_specs=[pl.BlockSpec((tq, LANES), lambda bi, hp, i: (bi * nq + i, hp)),
                  pl.BlockSpec((s, LANES), lambda bi, hp, i: (bi, hp)),
                  pl.BlockSpec((s, LANES), lambda bi, hp, i: (bi, hp)),
                  pl.BlockSpec((1, nblk, LANES), lambda bi, hp, i: (bi, 0, hp)),
                  pl.BlockSpec((1, LANES), lambda bi, hp, i: (0, hp))],
        out_specs=pl.BlockSpec((tq, LANES), lambda bi, hp, i: (bi * nq + i, hp)),
        scratch_shapes=[pltpu.VMEM((nblk, tq), F32), pltpu.VMEM((2 * tq, LANES), F32),
                        pltpu.VMEM((2 * tq, LANES), F32), pltpu.VMEM((2 * tq, LANES), F32)],
        compiler_params=_params(3),
        name="moba",
    )(qa, kab, vab, kmean, g_out)


def _sb_kernel(q_ref, k_ref, v_ref, go_ref, o_ref, acc_ref, car_ref, *, tq):
    i = pl.program_id(2)
    q = q_ref[...]
    lane = lax.broadcasted_iota(jnp.int32, (1, LANES), 1)
    q2 = jnp.concatenate([jnp.where((lane // HEAD_DIM) == hh, q, jnp.zeros_like(q))
                          for hh in range(HEADS_PER_TILE)], axis=0)
    row_in_tile = lax.broadcasted_iota(jnp.int32, (2 * tq, 1), 0) & (tq - 1)
    kofs = lax.broadcasted_iota(jnp.int32, (1, MOBA_BLOCK), 1)
    later = _later_matrix(MOBA_BLOCK)

    def step(j, first):
        start = pl.multiple_of(j * MOBA_BLOCK, MOBA_BLOCK)
        ks = k_ref[pl.ds(start, MOBA_BLOCK), :]
        vs = v_ref[pl.ds(start, MOBA_BLOCK), :]
        z = _dot_nt(q2, ks)
        sp = _softplus(z)
        if first:
            vis = kofs < row_in_tile
            spk = jnp.where(vis, sp, 0.0)
        else:
            spk = sp
        w = jnp.exp(z - sp - _dot(spk.astype(BF16), later))
        if first:
            w = jnp.where(vis, w, 0.0)
        pv = _dot(w.astype(BF16), vs)
        drop = jnp.sum(spk, axis=1, keepdims=True)
        if first:
            acc_ref[...] = pv
            car_ref[...] = jnp.broadcast_to(-drop, car_ref.shape)
        else:
            car = car_ref[...]
            acc_ref[...] += jnp.exp(car) * pv
            car_ref[...] = car - drop

    step(i, True)

    def cond(state):
        t, car_max = state
        return jnp.logical_and(t < i, car_max > UNDERFLOW_LOG)

    def body(state):
        t, _ = state
        step(i - 1 - t, False)
        return t + 1, jnp.max(car_ref[...])

    lax.while_loop(cond, body, (jnp.int32(0), jnp.max(car_ref[...])))
    o_ref[...] = _merge_heads_and_norm(acc_ref[...], go_ref[...], tq).astype(o_ref.dtype)


def _sb_call(qb, kbb, vbb, g_out, b, s):
    tq = MOBA_BLOCK
    nq = s // tq
    n_tiles = qb.shape[1] // LANES
    return pl.pallas_call(
        functools.partial(_sb_kernel, tq=tq),
        out_shape=jax.ShapeDtypeStruct(qb.shape, BF16),
        grid=(b, n_tiles, nq),
        in_specs=[pl.BlockSpec((tq, LANES), lambda bi, hp, i: (bi * nq + i, hp)),
                  pl.BlockSpec((s, LANES), lambda bi, hp, i: (bi, hp)),
                  pl.BlockSpec((s, LANES), lambda bi, hp, i: (bi, hp)),
                  pl.BlockSpec((1, LANES), lambda bi, hp, i: (0, hp))],
        out_specs=pl.BlockSpec((tq, LANES), lambda bi, hp, i: (bi * nq + i, hp)),
        scratch_shapes=[pltpu.VMEM((2 * tq, LANES), F32), pltpu.VMEM((2 * tq, LANES), F32)],
        compiler_params=_params(3),
        name="sb",
    )(qb, kbb, vbb, g_out)


def _block_diag_queries(q, n_heads):
    t, w = q.shape
    head_of_lane = lax.broadcasted_iota(jnp.int32, (n_heads, w), 1) // HEAD_DIM
    hmask = head_of_lane == lax.broadcasted_iota(jnp.int32, (n_heads, w), 0)
    rows = [jnp.where(hmask, jnp.broadcast_to(q[ti:ti + 1, :], (n_heads, w)), 0.0) for ti in range(t)]
    return jnp.concatenate(rows, axis=0), jnp.concatenate([hmask] * t, axis=0)


def _collapse_heads(o, hmask_rows, gain, n_tok, n_heads):
    oh = jnp.where(hmask_rows, o, 0.0)
    ss = jnp.sum(oh * oh, axis=1, keepdims=True) * (1.0 / HEAD_DIM)
    of = oh * lax.rsqrt(ss + LN_EPS)
    rows = [jnp.sum(of[ti * n_heads:(ti + 1) * n_heads, :], axis=0, keepdims=True) for ti in range(n_tok)]
    return jnp.concatenate(rows, axis=0) * gain


def _smoba_kernel(pt_ref, q_ref, kn_ref, vn_ref, go_ref, *rest, n_pages, n_tok, n_heads):
    del pt_ref
    k_pages, v_pages, o_ref = rest[:n_pages], rest[n_pages:2 * n_pages], rest[2 * n_pages]
    qbd, hmask_rows = _block_diag_queries(q_ref[0].astype(F32), n_heads)
    qbd_bf = qbd.astype(BF16)
    kn, vn = kn_ref[0], vn_ref[0]
    tok_of_row = lax.broadcasted_iota(jnp.int32, (n_tok * n_heads, 1), 0) // n_heads
    pages_per_block = MOBA_BLOCK // PAGE_SIZE
    n_blocks = n_pages // pages_per_block
    s_pages = [_dot(qbd_bf, k_pages[p][...].astype(BF16)) for p in range(n_pages)]
    gates = [sum(jnp.sum(s_pages[n * pages_per_block + u], axis=1, keepdims=True) for u in range(pages_per_block))
             for n in range(n_blocks)]
    biases = []
    for n in range(n_blocks):
        rank = jnp.zeros_like(gates[n])
        for n2 in range(n_blocks):
            if n2 != n:
                beats = (gates[n2] > gates[n]) | ((gates[n2] == gates[n]) & (n2 < n))
                rank = rank + beats.astype(F32)
        biases.append(jnp.where(rank < min(MOBA_TOPK, n_blocks), 0.0, NEG))
    s_new = [jnp.where(t <= tok_of_row, jnp.sum(qbd * kn[t:t + 1, :], axis=1, keepdims=True), NEG)
             for t in range(n_tok)]
    s_sel = [s_pages[p] + biases[p // pages_per_block] for p in range(n_pages)]
    m = functools.reduce(jnp.maximum, s_new)
    for p in range(n_pages):
        m = jnp.maximum(m, jnp.max(s_sel[p], axis=1, keepdims=True))
    acc = jnp.zeros(qbd.shape, F32)
    l = jnp.zeros_like(m)
    for t in range(n_tok):
        pt = jnp.exp(s_new[t] - m)
        l = l + pt
        acc = acc + pt * vn[t:t + 1, :]
    for p in range(n_pages):
        pp = jnp.exp(s_sel[p] - m)
        l = l + jnp.sum(pp, axis=1, keepdims=True)
        acc = acc + _dot_nt(pp.astype(BF16), v_pages[p][...].astype(BF16))
    o_ref[0] = _collapse_heads(acc / l, hmask_rows, go_ref[...], n_tok, n_heads).astype(o_ref.dtype)


def _ssb_kernel(pt_ref, q_ref, kn_ref, vn_ref, go_ref, *rest, n_pages, n_tok, n_heads):
    del pt_ref
    k_pages, v_pages = rest[:n_pages], rest[n_pages:2 * n_pages]
    o_ref, acc_ref, car_ref = rest[2 * n_pages:]
    qbd, hmask_rows = _block_diag_queries(q_ref[0].astype(F32), n_heads)
    qbd_bf = qbd.astype(BF16)
    kn, vn = kn_ref[0], vn_ref[0]
    tok_of_row = lax.broadcasted_iota(jnp.int32, (n_tok * n_heads, 1), 0) // n_heads
    later = _later_matrix(PAGE_SIZE)
    acc = jnp.zeros(qbd.shape, F32)
    carry = jnp.zeros((n_tok * n_heads, 1), F32)
    for t in reversed(range(n_tok)):
        z = jnp.sum(qbd * kn[t:t + 1, :], axis=1, keepdims=True)
        sp = _softplus(z)
        vis = t < tok_of_row
        acc = acc + jnp.where(vis, jnp.exp(z - sp + carry), 0.0) * vn[t:t + 1, :]
        carry = carry - jnp.where(vis, sp, 0.0)
    acc_ref[...] = acc
    car_ref[...] = jnp.broadcast_to(carry, car_ref.shape)
    for p in reversed(range(n_pages)):
        @pl.when(jnp.max(car_ref[...]) > UNDERFLOW_LOG)
        def _():
            z = _dot(qbd_bf, k_pages[p][...].astype(BF16))
            sp = _softplus(z)
            w = jnp.exp(z - sp - _dot(sp.astype(BF16), later))
            car = car_ref[...]
            acc_ref[...] += jnp.exp(car[:, :1]) * _dot_nt(w.astype(BF16), v_pages[p][...].astype(BF16))
            car_ref[...] = car - jnp.sum(sp, axis=1, keepdims=True)
    o_ref[0] = _collapse_heads(acc_ref[...], hmask_rows, go_ref[...], n_tok, n_heads).astype(o_ref.dtype)


def _sample_attn_call(body, name, page_table, q, k_new, v_new, g_out, cache_k, cache_v, scratch_shapes=()):
    db, n_pages = page_table.shape
    _, t, w = q.shape
    n_heads = w // HEAD_DIM
    tok = lambda: pl.BlockSpec((1, t, w), lambda bi, pt: (bi, 0, 0))
    page = lambda p: pl.BlockSpec((None, w, PAGE_SIZE), lambda bi, pt: (pt[bi, p], 0, 0))
    grid_spec = pltpu.PrefetchScalarGridSpec(
        num_scalar_prefetch=1,
        grid=(db,),
        in_specs=[tok(), tok(), tok(), pl.BlockSpec((1, w), lambda bi, pt: (0, 0))]
                 + [page(p) for p in range(n_pages)] * 2,
        out_specs=tok(),
        scratch_shapes=scratch_shapes,
    )
    return pl.pallas_call(
        functools.partial(body, n_pages=n_pages, n_tok=t, n_heads=n_heads),
        out_shape=jax.ShapeDtypeStruct(q.shape, BF16),
        grid_spec=grid_spec,
        compiler_params=_params(1),
        name=name,
    )(page_table, q, k_new, v_new, g_out, *([cache_k] * n_pages), *([cache_v] * n_pages))


def _post_kernel(x_ref, g0_ref, b0_ref, ma_ref, mb_ref, mod_ref, wo_ref, g1_ref, b1_ref,
                 wg_ref, wu_ref, wd_ref, g2_ref, b2_ref, o_ref, *, alpha, wa, ff_chunk):
    xn = _layer_norm(x_ref[...], g0_ref[...], b0_ref[...])
    attn = _dot(ma_ref[...], wo_ref[:wa, :]) + _dot(mb_ref[...], wo_ref[wa:, :])
    x1 = _layer_norm(alpha * xn + mod_ref[2] * attn, g1_ref[...], b1_ref[...])
    hb = (x1 * (1.0 + mod_ref[4]) + mod_ref[3]).astype(BF16)
    f = jnp.zeros(x1.shape, F32)
    for c in range(0, wg_ref.shape[1], ff_chunk):
        gate = _dot(hb, wg_ref[:, c:c + ff_chunk])
        up = _dot(hb, wu_ref[:, c:c + ff_chunk])
        act = gate / (1.0 + jnp.exp(-gate)) * up
        f = f + _dot(act.astype(BF16), wd_ref[c:c + ff_chunk, :])
    o_ref[...] = _layer_norm(alpha * x1 + mod_ref[5] * f, g2_ref[...], b2_ref[...])


def _post_call(x2, g0, b0, ma, mb, mods, mod_spec, wo, g1, b1, wg, wu, wd, g2, b2, alpha, tm, ff_chunk):
    r, d = x2.shape
    wa = ma.shape[1]
    row = lambda w: pl.BlockSpec((tm, w), lambda i: (i, 0))
    vec = lambda: _const_spec((1, d))
    return pl.pallas_call(
        functools.partial(_post_kernel, alpha=alpha, wa=wa, ff_chunk=ff_chunk),
        out_shape=jax.ShapeDtypeStruct((r, d), F32),
        grid=(r // tm,),
        in_specs=[row(d), vec(), vec(), row(wa), row(mb.shape[1]), mod_spec, _const_spec(wo.shape), vec(), vec(),
                  _const_spec(wg.shape), _const_spec(wu.shape), _const_spec(wd.shape), vec(), vec()],
        out_specs=row(d),
        compiler_params=_params(1),
        name="post",
    )(x2, g0, b0, ma, mb, mods, wo, g1, b1, wg, wu, wd, g2, b2)


def _rotary_tables(pos):
    half = ROT_DIM // 2
    inv = ROPE_THETA ** (-jnp.arange(half, dtype=F32) * 2.0 / ROT_DIM)
    ang = pos.astype(F32)[:, None] * inv
    cos, sin = jnp.cos(ang), jnp.sin(ang)
    n = pos.shape[0]
    rest = HEAD_DIM - ROT_DIM
    c = jnp.concatenate([cos, cos, jnp.ones((n, rest), F32)], axis=1)
    s_up = jnp.concatenate([-sin, jnp.zeros((n, half + rest), F32)], axis=1)
    s_dn = jnp.concatenate([jnp.zeros((n, half), F32), sin, jnp.zeros((n, rest), F32)], axis=1)
    return jnp.stack([jnp.tile(t, (1, HEADS_PER_TILE)) for t in (c, s_up, s_dn)])


def kernel(x_prompt, x_sample, cache_moba_k, cache_moba_v, cache_sb_k, cache_sb_v, page_table, c_prompt, c_sample,
           ln0_g, ln0_b, w_mod, b_mod, w_in, g_out_a, g_out_b, w_out, ln1_g, ln1_b, w_gate, w_up, w_down,
           ln2_g, ln2_b):
    b, s, d = x_prompt.shape
    db, t, _ = x_sample.shape
    depth = w_mod.shape[0]
    assert depth == 1, "ln0 is fused into the first layer's kernels; only a one-layer trunk is supported"
    wa, wb = g_out_a.shape[1], g_out_b.shape[1]
    ha, hb = wa // HEAD_DIM, wb // HEAD_DIM
    n_pages = page_table.shape[1]
    past = n_pages * PAGE_SIZE
    assert s % MOBA_BLOCK == 0 and past % MOBA_BLOCK == 0
    alpha = (2 * depth) ** 0.25
    vec = lambda a: a.reshape(1, -1)

    n_mod = 6
    rows = b + db
    pad = -rows % 8
    c_all = jnp.concatenate([c_prompt, c_sample, jnp.zeros((pad, d), F32)], axis=0)
    m = _mod_call(c_all, w_mod[0], b_mod[0].reshape(1, -1))
    mods_p = m[:b].reshape(b, n_mod, 1, d)
    mods_s = jnp.repeat(m[b:rows].reshape(db, n_mod, d), t, axis=0).transpose(1, 0, 2)

    w_in_bf = w_in[0].astype(BF16)
    wo_bf, wg_bf, wu_bf, wd_bf = (w[0].astype(BF16) for w in (w_out, w_gate, w_up, w_down))
    g0, b0 = vec(ln0_g), vec(ln0_b)
    ln = [vec(a[0]) for a in (ln1_g, ln1_b, ln2_g, ln2_b)]
    goa, gob = g_out_a[0].reshape(1, wa), g_out_b[0].reshape(1, wb)

    tm = 512
    ff_chunk = 1408

    xp2 = x_prompt.reshape(b * s, d)
    mod_spec_p = pl.BlockSpec((None, n_mod, 1, d), lambda i: (i // (s // tm), 0, 0, 0))
    rot_p = _rotary_tables(jnp.arange(s, dtype=jnp.int32))
    (qa, qb, ka, va, kb, vb, kab, vab, kbb, vbb, kmean) = _pre_call(
        xp2, g0, b0, mods_p, mod_spec_p, rot_p, w_in_bf, wa, wb, tm, True)
    mixed_a = _moba_call(qa, kab, vab, kmean.reshape(b, s // MOBA_BLOCK, wa), goa, b, s)
    mixed_b = _sb_call(qb, kbb, vbb, gob, b, s)
    y_p = _post_call(xp2, g0, b0, mixed_a, mixed_b, mods_p, mod_spec_p, wo_bf, ln[0], ln[1],
                     wg_bf, wu_bf, wd_bf, ln[2], ln[3], alpha, tm, ff_chunk)

    rs = db * t
    xs2 = x_sample.reshape(rs, d)
    mod_spec_s = pl.BlockSpec((n_mod, rs, d), lambda i: (0, 0, 0))
    rot_s = _rotary_tables(jnp.tile(past + jnp.arange(t, dtype=jnp.int32), db))
    (sqa, sqb, ska, sva, skb, svb) = _pre_call(
        xs2, g0, b0, mods_s, mod_spec_s, rot_s, w_in_bf, wa, wb, rs, False)
    tok3 = lambda a: a.reshape(db, t, a.shape[-1])
    pages = lambda c: c[0].transpose(0, 2, 3, 1).reshape(c.shape[1], -1, PAGE_SIZE)
    smixed_a = _sample_attn_call(_smoba_kernel, "smoba", page_table, tok3(sqa), tok3(ska), tok3(sva), goa,
                                 pages(cache_moba_k), pages(cache_moba_v))
    smixed_b = _sample_attn_call(_ssb_kernel, "ssb", page_table, tok3(sqb), tok3(skb), tok3(svb), gob,
                                 pages(cache_sb_k), pages(cache_sb_v),
                                 scratch_shapes=[pltpu.VMEM((t * hb, wb), F32), pltpu.VMEM((t * hb, LANES), F32)])
    y_s = _post_call(xs2, g0, b0, smixed_a.reshape(rs, wa), smixed_b.reshape(rs, wb), mods_s, mod_spec_s,
                     wo_bf, ln[0], ln[1], wg_bf, wu_bf, wd_bf, ln[2], ln[3], alpha, rs, ff_chunk)

    return (y_p.reshape(b, s, d), y_s.reshape(db, t, d),
            ka.reshape(1, b, s, ha, HEAD_DIM), va.reshape(1, b, s, ha, HEAD_DIM),
            kb.reshape(1, b, s, hb, HEAD_DIM), vb.reshape(1, b, s, hb, HEAD_DIM),
            ska.reshape(1, db, t, ha, HEAD_DIM), sva.reshape(1, db, t, ha, HEAD_DIM),
            skb.reshape(1, db, t, hb, HEAD_DIM), svb.reshape(1, db, t, hb, HEAD_DIM))
```

```python
import functools

import jax
import jax.numpy as jnp
from jax import lax
from jax.experimental import pallas as pl
from jax.experimental.pallas import tpu as pltpu

F32 = jnp.float32
BF16 = jnp.bfloat16

HEAD_DIM = 64
MOBA_BLOCK = 256
MOBA_TOPK = 3
PAGE_SIZE = 128
ROPE_THETA = 500000.0
ROT_DIM = HEAD_DIM // 4
LN_EPS = 1e-5
LANES = 128
HEADS_PER_TILE = LANES // HEAD_DIM
NEG = -1e30
UNDERFLOW_LOG = -110.0
VMEM_LIMIT = 56 * 1024 * 1024


def _dot(a, b):
    return jnp.dot(a, b, preferred_element_type=F32)


def _dot_nt(a, b):
    return lax.dot_general(a, b, (((1,), (1,)), ((), ())), preferred_element_type=F32)


def _dot_tn(a, b):
    return lax.dot_general(a, b, (((0,), (0,)), ((), ())), preferred_element_type=F32)


def _split_bf16(x):
    hi = x.astype(BF16)
    lo = (x - hi.astype(F32)).astype(BF16)
    return hi, lo


def _layer_norm(x, g, b):
    mu = jnp.mean(x, axis=-1, keepdims=True)
    xc = x - mu
    var = jnp.mean(xc * xc, axis=-1, keepdims=True)
    return xc * lax.rsqrt(var + LN_EPS) * g + b


def _softplus(z):
    return jnp.maximum(z, 0.0) + jnp.log(1.0 + jnp.exp(-jnp.abs(z)))


def _later_matrix(n):
    return (lax.broadcasted_iota(jnp.int32, (n, n), 0) > lax.broadcasted_iota(jnp.int32, (n, n), 1)).astype(BF16)


def _params(n_grid):
    return pltpu.CompilerParams(dimension_semantics=("arbitrary",) * n_grid, vmem_limit_bytes=VMEM_LIMIT)


def _const_spec(shape):
    zeros = (0,) * len(shape)
    return pl.BlockSpec(shape, lambda *_: zeros, pipeline_mode=pl.Buffered(1))


def _mod_kernel(c_ref, w_ref, b_ref, o_ref):
    c = c_ref[...]
    a = c / (1.0 + jnp.exp(-c))
    o_ref[...] = _dot(a.astype(BF16), w_ref[...].astype(BF16)) + b_ref[...]


def _mod_call(c_all, w_mod, b_mod, tn=1536):
    r, d = c_all.shape
    n = w_mod.shape[1]
    return pl.pallas_call(
        _mod_kernel,
        out_shape=jax.ShapeDtypeStruct((r, n), F32),
        grid=(n // tn,),
        in_specs=[pl.BlockSpec((r, d), lambda j: (0, 0)),
                  pl.BlockSpec((d, tn), lambda j: (0, j)),
                  pl.BlockSpec((1, tn), lambda j: (0, j))],
        out_specs=pl.BlockSpec((r, tn), lambda j: (0, j)),
        compiler_params=_params(1),
        name="mod",
    )(c_all, w_mod, b_mod)


def _pre_kernel(x_ref, g0_ref, b0_ref, mod_ref, rot_ref, w_ref, qa_ref, qb_ref, ka_ref, va_ref, kb_ref, vb_ref,
                *extra_refs, wa, wb):
    xn = _layer_norm(x_ref[...], g0_ref[...], b0_ref[...])
    h = xn * (1.0 + mod_ref[1]) + mod_ref[0]
    qkv = _dot(h.astype(BF16), w_ref[...])
    cos, s_up, s_dn = rot_ref[0], rot_ref[1], rot_ref[2]

    def rotate(t):
        return t * cos + pltpu.roll(t, LANES - ROT_DIM // 2, 1) * s_up + pltpu.roll(t, ROT_DIM // 2, 1) * s_dn

    for t in range(wa // LANES):
        sl = slice(t * LANES, (t + 1) * LANES)
        qa = rotate(qkv[:, sl])
        ka = rotate(qkv[:, wa + t * LANES:wa + (t + 1) * LANES])
        qa_ref[:, sl] = (qa * HEAD_DIM ** -0.5).astype(BF16)
        ka_ref[:, sl] = ka
        if extra_refs:
            kab_ref, km_ref = extra_refs[0], extra_refs[4]
            kab_ref[:, sl] = ka.astype(BF16)
            for g in range(km_ref.shape[0]):
                blk = ka[g * MOBA_BLOCK:(g + 1) * MOBA_BLOCK, :]
                km_ref[g, :, sl] = jnp.sum(blk, axis=0, keepdims=True) * (1.0 / MOBA_BLOCK)
    o = 3 * wa
    va, kb, vb = qkv[:, 2 * wa:o], qkv[:, o + wb:o + 2 * wb], qkv[:, o + 2 * wb:o + 3 * wb]
    qb_ref[...] = (qkv[:, o:o + wb] * HEAD_DIM ** -0.5).astype(BF16)
    va_ref[...] = va
    kb_ref[...] = kb
    vb_ref[...] = vb
    if extra_refs:
        extra_refs[1][...] = va.astype(BF16)
        extra_refs[2][...] = kb.astype(BF16)
        extra_refs[3][...] = vb.astype(BF16)


def _pre_call(x2, g0, b0, mods, mod_spec, rot, w_in_bf, wa, wb, tm, prompt):
    r, d = x2.shape
    n_rot = rot.shape[1] // tm
    row = lambda w: pl.BlockSpec((tm, w), lambda i: (i, 0))
    outs = [jax.ShapeDtypeStruct((r, wa), BF16), jax.ShapeDtypeStruct((r, wb), BF16),
            jax.ShapeDtypeStruct((r, wa), F32), jax.ShapeDtypeStruct((r, wa), F32),
            jax.ShapeDtypeStruct((r, wb), F32), jax.ShapeDtypeStruct((r, wb), F32)]
    out_specs = [row(wa), row(wb), row(wa), row(wa), row(wb), row(wb)]
    if prompt:
        outs += [jax.ShapeDtypeStruct((r, wa), BF16), jax.ShapeDtypeStruct((r, wa), BF16),
                 jax.ShapeDtypeStruct((r, wb), BF16), jax.ShapeDtypeStruct((r, wb), BF16),
                 jax.ShapeDtypeStruct((r // MOBA_BLOCK, 1, wa), F32)]
        out_specs += [row(wa), row(wa), row(wb), row(wb),
                      pl.BlockSpec((tm // MOBA_BLOCK, 1, wa), lambda i: (i, 0, 0))]
    return pl.pallas_call(
        functools.partial(_pre_kernel, wa=wa, wb=wb),
        out_shape=outs,
        grid=(r // tm,),
        in_specs=[row(d), _const_spec((1, d)), _const_spec((1, d)), mod_spec,
                  pl.BlockSpec((3, tm, LANES), lambda i: (0, i % n_rot, 0)),
                  _const_spec(w_in_bf.shape)],
        out_specs=out_specs,
        compiler_params=_params(1),
        name="pre",
    )(x2, g0, b0, mods, rot, w_in_bf)


def _merge_heads_and_norm(o2, gain, tq):
    lane = lax.broadcasted_iota(jnp.int32, (1, LANES), 1)
    first = lane < HEAD_DIM
    o = jnp.where(first, o2[:tq], o2[tq:])
    sq = o * o
    ss0 = jnp.sum(jnp.where(first, sq, 0.0), axis=1, keepdims=True)
    ss1 = jnp.sum(jnp.where(first, 0.0, sq), axis=1, keepdims=True)
    scale = jnp.where(first, lax.rsqrt(ss0 * (1.0 / HEAD_DIM) + LN_EPS), lax.rsqrt(ss1 * (1.0 / HEAD_DIM) + LN_EPS))
    return o * scale * gain


def _moba_kernel(q_ref, k_ref, v_ref, km_ref, go_ref, o_ref, g_ref, q2_ref, acc_ref, m_ref, l_ref, *, tq, nblk, n_chain):
    i = pl.program_id(2)
    lane = lax.broadcasted_iota(jnp.int32, (1, LANES), 1)
    n_idx = lax.broadcasted_iota(jnp.int32, (nblk, tq), 0)
    valid = n_idx < i
    place = (lax.broadcasted_iota(jnp.int32, (nblk, LANES), 1)
             == lax.broadcasted_iota(jnp.int32, (nblk, LANES), 0)).astype(BF16)
    for c in range(n_chain):
        tile = slice(c * LANES, (c + 1) * LANES)
        q = q_ref[:, tile]
        km_hi, km_lo = _split_bf16(km_ref[0, :, tile])
        for hh in range(HEADS_PER_TILE):
            qh = jnp.where((lane // HEAD_DIM) == hh, q, jnp.zeros_like(q))
            g = jnp.where(valid, _dot_nt(km_hi, qh) + _dot_nt(km_lo, qh), -jnp.inf)
            g_ref[...] = g

            def rank_step(n2, rank):
                row = g_ref[pl.ds(n2, 1), :]
                beats = (row > g) | ((row == g) & (n2 < n_idx))
                return rank + beats.astype(jnp.int32)

            rank = lax.fori_loop(0, i, rank_step, jnp.zeros((nblk, tq), jnp.int32))
            sel = (valid & (rank < MOBA_TOPK)) | (n_idx == i)
            bias_t = jnp.where(sel, 0.0, NEG).astype(BF16)
            rows = slice(hh * tq, (hh + 1) * tq)
            q2_ref[c, rows, :LANES] = qh
            q2_ref[c, rows, LANES:] = _dot_tn(bias_t, place).astype(BF16)
    row_in_tile = lax.broadcasted_iota(jnp.int32, (2 * tq, 1), 0) & (tq - 1)
    kofs = lax.broadcasted_iota(jnp.int32, (1, MOBA_BLOCK), 1)

    acc_ref[...] = jnp.zeros_like(acc_ref)
    m_ref[...] = jnp.full_like(m_ref, NEG)
    l_ref[...] = jnp.zeros_like(l_ref)

    def step(j, causal):
        start = pl.multiple_of(j * MOBA_BLOCK, MOBA_BLOCK)
        onehot = jnp.broadcast_to((lane == j).astype(BF16), (MOBA_BLOCK, LANES))
        scores = []
        for c in range(n_chain):
            ks = k_ref[pl.ds(start, MOBA_BLOCK), c * LANES:(c + 1) * LANES]
            scores.append(_dot_nt(q2_ref[c], jnp.concatenate([ks, onehot], axis=1)))
        for c in range(n_chain):
            vs = v_ref[pl.ds(start, MOBA_BLOCK), c * LANES:(c + 1) * LANES]
            s = scores[c]
            if causal:
                s = jnp.where(kofs <= row_in_tile, s, NEG)
            m_prev = m_ref[c]
            m_new = jnp.maximum(m_prev, jnp.max(s, axis=1, keepdims=True))
            alpha = jnp.exp(m_prev - m_new)
            p = jnp.exp(s - jnp.concatenate([m_new] * (MOBA_BLOCK // LANES), axis=1))
            l_ref[c] = alpha * l_ref[c] + sum(p[:, u * LANES:(u + 1) * LANES] for u in range(MOBA_BLOCK // LANES))
            acc_ref[c] = alpha * acc_ref[c] + _dot(p.astype(BF16), vs)
            m_ref[c] = m_new

    def past_step(j, carry):
        step(j, False)
        return carry

    lax.fori_loop(0, i, past_step, 0)
    step(i, True)
    for c in range(n_chain):
        tile = slice(c * LANES, (c + 1) * LANES)
        o = acc_ref[c] / jnp.sum(l_ref[c], axis=1, keepdims=True)
        o_ref[:, tile] = _merge_heads_and_norm(o, go_ref[:, tile], tq).astype(o_ref.dtype)


def _moba_call(qa, kab, vab, kmean, g_out, b, s, n_chain):
    tq = MOBA_BLOCK
    nq = s // tq
    nblk = s // MOBA_BLOCK
    w = n_chain * LANES
    n_groups = qa.shape[1] // w
    stats = pltpu.VMEM((n_chain, 2 * tq, LANES), F32)
    return pl.pallas_call(
        functools.partial(_moba_kernel, tq=tq, nblk=nblk, n_chain=n_chain),
        out_shape=jax.ShapeDtypeStruct(qa.shape, BF16),
        grid=(b, n_groups, nq),
        in_specs=[pl.BlockSpec((tq, w), lambda bi, hp, i: (bi * nq + i, hp)),
                  pl.BlockSpec((s, w), lambda bi, hp, i: (bi, hp)),
                  pl.BlockSpec((s, w), lambda bi, hp, i: (bi, hp)),
                  pl.BlockSpec((1, nblk, w), lambda bi, hp, i: (bi, 0, hp)),
                  pl.BlockSpec((1, w), lambda bi, hp, i: (0, hp))],
        out_specs=pl.BlockSpec((tq, w), lambda bi, hp, i: (bi * nq + i, hp)),
        scratch_shapes=[pltpu.VMEM((nblk, tq), F32), pltpu.VMEM((n_chain, 2 * tq, 2 * LANES), BF16),
                        stats, stats, stats],
        compiler_params=_params(3),
        name="moba",
    )(qa, kab, vab, kmean, g_out)


def _sb_kernel(q_ref, k_ref, v_ref, go_ref, o_ref, q2_ref, acc_ref, car_ref, *, tq, n_chain):
    i = pl.program_id(2)
    lane = lax.broadcasted_iota(jnp.int32, (1, LANES), 1)
    for c in range(n_chain):
        q = q_ref[:, c * LANES:(c + 1) * LANES]
        for hh in range(HEADS_PER_TILE):
            q2_ref[c, hh * tq:(hh + 1) * tq, :] = jnp.where((lane // HEAD_DIM) == hh, q, jnp.zeros_like(q))
    row_in_tile = lax.broadcasted_iota(jnp.int32, (2 * tq, 1), 0) & (tq - 1)
    kofs = lax.broadcasted_iota(jnp.int32, (1, MOBA_BLOCK), 1)
    later = _later_matrix(MOBA_BLOCK)

    def step(j, first):
        start = pl.multiple_of(j * MOBA_BLOCK, MOBA_BLOCK)
        vis = kofs < row_in_tile
        zs = [_dot_nt(q2_ref[c], k_ref[pl.ds(start, MOBA_BLOCK), c * LANES:(c + 1) * LANES]) for c in range(n_chain)]
        mids = []
        for c in range(n_chain):
            sp = _softplus(zs[c])
            spk = jnp.where(vis, sp, 0.0) if first else sp
            mids.append((zs[c] - sp, _dot(spk.astype(BF16), later), jnp.sum(spk, axis=1, keepdims=True)))
        for c in range(n_chain):
            log_beta, after, drop = mids[c]
            w = jnp.exp(log_beta - after)
            if first:
                w = jnp.where(vis, w, 0.0)
            pv = _dot(w.astype(BF16), v_ref[pl.ds(start, MOBA_BLOCK), c * LANES:(c + 1) * LANES])
            if first:
                acc_ref[c] = pv
                car_ref[c] = jnp.broadcast_to(-drop, pv.shape)
            else:
                car = car_ref[c]
                acc_ref[c] += jnp.exp(car) * pv
                car_ref[c] = car - drop

    step(i, True)

    def cond(state):
        t, car_max = state
        return jnp.logical_and(t < i, car_max > UNDERFLOW_LOG)

    def body(state):
        t, _ = state
        step(i - 1 - t, False)
        return t + 1, jnp.max(car_ref[...])

    lax.while_loop(cond, body, (jnp.int32(0), jnp.max(car_ref[...])))
    for c in range(n_chain):
        tile = slice(c * LANES, (c + 1) * LANES)
        o_ref[:, tile] = _merge_heads_and_norm(acc_ref[c], go_ref[:, tile], tq).astype(o_ref.dtype)


def _sb_call(qb, kbb, vbb, g_out, b, s, n_chain):
    tq = MOBA_BLOCK
    nq = s // tq
    w = n_chain * LANES
    n_groups = qb.shape[1] // w
    stats = pltpu.VMEM((n_chain, 2 * tq, LANES), F32)
    return pl.pallas_call(
        functools.partial(_sb_kernel, tq=tq, n_chain=n_chain),
        out_shape=jax.ShapeDtypeStruct(qb.shape, BF16),
        grid=(b, n_groups, nq),
        in_specs=[pl.BlockSpec((tq, w), lambda bi, hp, i: (bi * nq + i, hp)),
                  pl.BlockSpec((s, w), lambda bi, hp, i: (bi, hp)),
                  pl.BlockSpec((s, w), lambda bi, hp, i: (bi, hp)),
                  pl.BlockSpec((1, w), lambda bi, hp, i: (0, hp))],
        out_specs=pl.BlockSpec((tq, w), lambda bi, hp, i: (bi * nq + i, hp)),
        scratch_shapes=[pltpu.VMEM((n_chain, 2 * tq, LANES), BF16), stats, stats],
        compiler_params=_params(3),
        name="sb",
    )(qb, kbb, vbb, g_out)


def _block_diag_queries(q, n_heads):
    t, w = q.shape
    head_of_lane = lax.broadcasted_iota(jnp.int32, (n_heads, w), 1) // HEAD_DIM
    hmask = head_of_lane == lax.broadcasted_iota(jnp.int32, (n_heads, w), 0)
    rows = [jnp.where(hmask, jnp.broadcast_to(q[ti:ti + 1, :], (n_heads, w)), 0.0) for ti in range(t)]
    return jnp.concatenate(rows, axis=0), jnp.concatenate([hmask] * t, axis=0)


def _collapse_heads(o, hmask_rows, gain, n_tok, n_heads):
    oh = jnp.where(hmask_rows, o, 0.0)
    ss = jnp.sum(oh * oh, axis=1, keepdims=True) * (1.0 / HEAD_DIM)
    of = oh * lax.rsqrt(ss + LN_EPS)
    rows = [jnp.sum(of[ti * n_heads:(ti + 1) * n_heads, :], axis=0, keepdims=True) for ti in range(n_tok)]
    return jnp.concatenate(rows, axis=0) * gain


def _smoba_kernel(pt_ref, q_ref, kn_ref, vn_ref, go_ref, *rest, n_pages, n_tok, n_heads):
    del pt_ref
    k_pages, v_pages, o_ref = rest[:n_pages], rest[n_pages:2 * n_pages], rest[2 * n_pages]
    qbd, hmask_rows = _block_diag_queries(q_ref[0].astype(F32), n_heads)
    qbd_bf = qbd.astype(BF16)
    kn, vn = kn_ref[0], vn_ref[0]
    tok_of_row = lax.broadcasted_iota(jnp.int32, (n_tok * n_heads, 1), 0) // n_heads
    pages_per_block = MOBA_BLOCK // PAGE_SIZE
    n_blocks = n_pages // pages_per_block
    s_pages = [_dot(qbd_bf, k_pages[p][...].astype(BF16)) for p in range(n_pages)]
    gates = [sum(jnp.sum(s_pages[n * pages_per_block + u], axis=1, keepdims=True) for u in range(pages_per_block))
             for n in range(n_blocks)]
    biases = []
    for n in range(n_blocks):
        rank = jnp.zeros_like(gates[n])
        for n2 in range(n_blocks):
            if n2 != n:
                beats = (gates[n2] > gates[n]) | ((gates[n2] == gates[n]) & (n2 < n))
                rank = rank + beats.astype(F32)
        biases.append(jnp.where(rank < min(MOBA_TOPK, n_blocks), 0.0, NEG))
    s_new = [jnp.where(t <= tok_of_row, jnp.sum(qbd * kn[t:t + 1, :], axis=1, keepdims=True), NEG)
             for t in range(n_tok)]
    s_sel = [s_pages[p] + biases[p // pages_per_block] for p in range(n_pages)]
    m = functools.reduce(jnp.maximum, s_new)
    for p in range(n_pages):
        m = jnp.maximum(m, jnp.max(s_sel[p], axis=1, keepdims=True))
    acc = jnp.zeros(qbd.shape, F32)
    l = jnp.zeros_like(m)
    for t in range(n_tok):
        pt = jnp.exp(s_new[t] - m)
        l = l + pt
        acc = acc + pt * vn[t:t + 1, :]
    for p in range(n_pages):
        pp = jnp.exp(s_sel[p] - m)
        l = l + jnp.sum(pp, axis=1, keepdims=True)
        acc = acc + _dot_nt(pp.astype(BF16), v_pages[p][...].astype(BF16))
    o_ref[0] = _collapse_heads(acc / l, hmask_rows, go_ref[...], n_tok, n_heads).astype(o_ref.dtype)


def _ssb_kernel(pt_ref, q_ref, kn_ref, vn_ref, go_ref, *rest, n_pages, n_tok, n_heads):
    del pt_ref
    k_pages, v_pages = rest[:n_pages], rest[n_pages:2 * n_pages]
    o_ref, acc_ref, car_ref = rest[2 * n_pages:]
    qbd, hmask_rows = _block_diag_queries(q_ref[0].astype(F32), n_heads)
    qbd_bf = qbd.astype(BF16)
    kn, vn = kn_ref[0], vn_ref[0]
    tok_of_row = lax.broadcasted_iota(jnp.int32, (n_tok * n_heads, 1), 0) // n_heads
    later = _later_matrix(PAGE_SIZE)
    acc = jnp.zeros(qbd.shape, F32)
    carry = jnp.zeros((n_tok * n_heads, 1), F32)
    for t in reversed(range(n_tok)):
        z = jnp.sum(qbd * kn[t:t + 1, :], axis=1, keepdims=True)
        sp = _softplus(z)
        vis = t < tok_of_row
        acc = acc + jnp.where(vis, jnp.exp(z - sp + carry), 0.0) * vn[t:t + 1, :]
        carry = carry - jnp.where(vis, sp, 0.0)
    acc_ref[...] = acc
    car_ref[...] = jnp.broadcast_to(carry, car_ref.shape)
    for p in reversed(range(n_pages)):
        @pl.when(jnp.max(car_ref[...]) > UNDERFLOW_LOG)
        def _():
            z = _dot(qbd_bf, k_pages[p][...].astype(BF16))
            sp = _softplus(z)
            w = jnp.exp(z - sp - _dot(sp.astype(BF16), later))
            car = car_ref[...]
            acc_ref[...] += jnp.exp(car[:, :1]) * _dot_nt(w.astype(BF16), v_pages[p][...].astype(BF16))
            car_ref[...] = car - jnp.sum(sp, axis=1, keepdims=True)
    o_ref[0] = _collapse_heads(acc_ref[...], hmask_rows, go_ref[...], n_tok, n_heads).astype(o_ref.dtype)


def _sample_attn_call(body, name, page_table, q, k_new, v_new, g_out, cache_k, cache_v, scratch_shapes=()):
    db, n_pages = page_table.shape
    _, t, w = q.shape
    n_heads = w // HEAD_DIM
    tok = lambda: pl.BlockSpec((1, t, w), lambda bi, pt: (bi, 0, 0))
    page = lambda p: pl.BlockSpec((None, w, PAGE_SIZE), lambda bi, pt: (pt[bi, p], 0, 0))
    grid_spec = pltpu.PrefetchScalarGridSpec(
        num_scalar_prefetch=1,
        grid=(db,),
        in_specs=[tok(), tok(), tok(), pl.BlockSpec((1, w), lambda bi, pt: (0, 0))]
                 + [page(p) for p in range(n_pages)] * 2,
        out_specs=tok(),
        scratch_shapes=scratch_shapes,
    )
    return pl.pallas_call(
        functools.partial(body, n_pages=n_pages, n_tok=t, n_heads=n_heads),
        out_shape=jax.ShapeDtypeStruct(q.shape, BF16),
        grid_spec=grid_spec,
        compiler_params=_params(1),
        name=name,
    )(page_table, q, k_new, v_new, g_out, *([cache_k] * n_pages), *([cache_v] * n_pages))


def _post_kernel(x_ref, g0_ref, b0_ref, ma_ref, mb_ref, mod_ref, wo_ref, g1_ref, b1_ref,
                 wg_ref, wu_ref, wd_ref, g2_ref, b2_ref, o_ref, *, alpha, wa, ff_chunk):
    xn = _layer_norm(x_ref[...], g0_ref[...], b0_ref[...])
    attn = _dot(ma_ref[...], wo_ref[:wa, :]) + _dot(mb_ref[...], wo_ref[wa:, :])
    x1 = _layer_norm(alpha * xn + mod_ref[2] * attn, g1_ref[...], b1_ref[...])
    hb = (x1 * (1.0 + mod_ref[4]) + mod_ref[3]).astype(BF16)
    f = jnp.zeros(x1.shape, F32)
    for c in range(0, wg_ref.shape[1], ff_chunk):
        gate = _dot(hb, wg_ref[:, c:c + ff_chunk])
        up = _dot(hb, wu_ref[:, c:c + ff_chunk])
        act = gate / (1.0 + jnp.exp(-gate)) * up
        f = f + _dot(act.astype(BF16), wd_ref[c:c + ff_chunk, :])
    o_ref[...] = _layer_norm(alpha * x1 + mod_ref[5] * f, g2_ref[...], b2_ref[...])


def _post_call(x2, g0, b0, ma, mb, mods, mod_spec, wo, g1, b1, wg, wu, wd, g2, b2, alpha, tm, ff_chunk):
    r, d = x2.shape
    wa = ma.shape[1]
    row = lambda w: pl.BlockSpec((tm, w), lambda i: (i, 0))
    vec = lambda: _const_spec((1, d))
    return pl.pallas_call(
        functools.partial(_post_kernel, alpha=alpha, wa=wa, ff_chunk=ff_chunk),
        out_shape=jax.ShapeDtypeStruct((r, d), F32),
        grid=(r // tm,),
        in_specs=[row(d), vec(), vec(), row(wa), row(mb.shape[1]), mod_spec, _const_spec(wo.shape), vec(), vec(),
                  _const_spec(wg.shape), _const_spec(wu.shape), _const_spec(wd.shape), vec(), vec()],
        out_specs=row(d),
        compiler_params=_params(1),
        name="post",
    )(x2, g0, b0, ma, mb, mods, wo, g1, b1, wg, wu, wd, g2, b2)


def _rotary_tables(pos):
    half = ROT_DIM // 2
    inv = ROPE_THETA ** (-jnp.arange(half, dtype=F32) * 2.0 / ROT_DIM)
    ang = pos.astype(F32)[:, None] * inv
    cos, sin = jnp.cos(ang), jnp.sin(ang)
    n = pos.shape[0]
    rest = HEAD_DIM - ROT_DIM
    c = jnp.concatenate([cos, cos, jnp.ones((n, rest), F32)], axis=1)
    s_up = jnp.concatenate([-sin, jnp.zeros((n, half + rest), F32)], axis=1)
    s_dn = jnp.concatenate([jnp.zeros((n, half), F32), sin, jnp.zeros((n, rest), F32)], axis=1)
    return jnp.stack([jnp.tile(t, (1, HEADS_PER_TILE)) for t in (c, s_up, s_dn)])


def kernel(x_prompt, x_sample, cache_moba_k, cache_moba_v, cache_sb_k, cache_sb_v, page_table, c_prompt, c_sample,
           ln0_g, ln0_b, w_mod, b_mod, w_in, g_out_a, g_out_b, w_out, ln1_g, ln1_b, w_gate, w_up, w_down,
           ln2_g, ln2_b):
    b, s, d = x_prompt.shape
    db, t, _ = x_sample.shape
    depth = w_mod.shape[0]
    assert depth == 1, "ln0 is fused into the first layer's kernels; only a one-layer trunk is supported"
    wa, wb = g_out_a.shape[1], g_out_b.shape[1]
    ha, hb = wa // HEAD_DIM, wb // HEAD_DIM
    n_pages = page_table.shape[1]
    past = n_pages * PAGE_SIZE
    assert s % MOBA_BLOCK == 0 and past % MOBA_BLOCK == 0
    alpha = (2 * depth) ** 0.25
    vec = lambda a: a.reshape(1, -1)

    n_mod = 6
    rows = b + db
    pad = -rows % 8
    c_all = jnp.concatenate([c_prompt, c_sample, jnp.zeros((pad, d), F32)], axis=0)
    m = _mod_call(c_all, w_mod[0], b_mod[0].reshape(1, -1))
    mods_p = m[:b].reshape(b, n_mod, 1, d)
    mods_s = jnp.repeat(m[b:rows].reshape(db, n_mod, d), t, axis=0).transpose(1, 0, 2)

    w_in_bf = w_in[0].astype(BF16)
    wo_bf, wg_bf, wu_bf, wd_bf = (w[0].astype(BF16) for w in (w_out, w_gate, w_up, w_down))
    g0, b0 = vec(ln0_g), vec(ln0_b)
    ln = [vec(a[0]) for a in (ln1_g, ln1_b, ln2_g, ln2_b)]
    goa, gob = g_out_a[0].reshape(1, wa), g_out_b[0].reshape(1, wb)

    tm = 512
    ff_chunk = 1408

    xp2 = x_prompt.reshape(b * s, d)
    mod_spec_p = pl.BlockSpec((None, n_mod, 1, d), lambda i: (i // (s // tm), 0, 0, 0))
    rot_p = _rotary_tables(jnp.arange(s, dtype=jnp.int32))
    (qa, qb, ka, va, kb, vb, kab, vab, kbb, vbb, kmean) = _pre_call(
        xp2, g0, b0, mods_p, mod_spec_p, rot_p, w_in_bf, wa, wb, tm, True)
    mixed_a = _moba_call(qa, kab, vab, kmean.reshape(b, s // MOBA_BLOCK, wa), goa, b, s, n_chain=4)
    mixed_b = _sb_call(qb, kbb, vbb, gob, b, s, n_chain=4)
    y_p = _post_call(xp2, g0, b0, mixed_a, mixed_b, mods_p, mod_spec_p, wo_bf, ln[0], ln[1],
                     wg_bf, wu_bf, wd_bf, ln[2], ln[3], alpha, tm, ff_chunk)

    rs = db * t
    xs2 = x_sample.reshape(rs, d)
    mod_spec_s = pl.BlockSpec((n_mod, rs, d), lambda i: (0, 0, 0))
    rot_s = _rotary_tables(jnp.tile(past + jnp.arange(t, dtype=jnp.int32), db))
    (sqa, sqb, ska, sva, skb, svb) = _pre_call(
        xs2, g0, b0, mods_s, mod_spec_s, rot_s, w_in_bf, wa, wb, rs, False)
    tok3 = lambda a: a.reshape(db, t, a.shape[-1])
    pages = lambda c: c[0].transpose(0, 2, 3, 1).reshape(c.shape[1], -1, PAGE_SIZE)
    smixed_a = _sample_attn_call(_smoba_kernel, "smoba", page_table, tok3(sqa), tok3(ska), tok3(sva), goa,
                                 pages(cache_moba_k), pages(cache_moba_v))
    smixed_b = _sample_attn_call(_ssb_kernel, "ssb", page_table, tok3(sqb), tok3(skb), tok3(svb), gob,
                                 pages(cache_sb_k), pages(cache_sb_v),
                                 scratch_shapes=[pltpu.VMEM((t * hb, wb), F32), pltpu.VMEM((t * hb, LANES), F32)])
    y_s = _post_call(xs2, g0, b0, smixed_a.reshape(rs, wa), smixed_b.reshape(rs, wb), mods_s, mod_spec_s,
                     wo_bf, ln[0], ln[1], wg_bf, wu_bf, wd_bf, ln[2], ln[3], alpha, rs, ff_chunk)

    return (y_p.reshape(b, s, d), y_s.reshape(db, t, d),
            ka.reshape(1, b, s, ha, HEAD_DIM), va.reshape(1, b, s, ha, HEAD_DIM),
            kb.reshape(1, b, s, hb, HEAD_DIM), vb.reshape(1, b, s, hb, HEAD_DIM),
            ska.reshape(1, db, t, ha, HEAD_DIM), sva.reshape(1, db, t, ha, HEAD_DIM),
            skb.reshape(1, db, t, hb, HEAD_DIM), svb.reshape(1, db, t, hb, HEAD_DIM))
```

```python
import functools

import jax
import jax.numpy as jnp
from jax import lax
from jax.experimental import pallas as pl
from jax.experimental.pallas import tpu as pltpu

F32 = jnp.float32
BF16 = jnp.bfloat16

HEAD_DIM = 64
MOBA_BLOCK = 256
MOBA_TOPK = 3
PAGE_SIZE = 128
ROPE_THETA = 500000.0
ROT_DIM = HEAD_DIM // 4
LN_EPS = 1e-5
LANES = 128
HEADS_PER_TILE = LANES // HEAD_DIM
NEG = -1e30
UNDERFLOW_LOG = -110.0
SB_RECENT_PAGES = 2
SB_GROUP = 4
VMEM_LIMIT = 56 * 1024 * 1024


def _dot(a, b):
    return jnp.dot(a, b, preferred_element_type=F32)


def _dot_nt(a, b):
    return lax.dot_general(a, b, (((1,), (1,)), ((), ())), preferred_element_type=F32)


def _dot_tn(a, b):
    return lax.dot_general(a, b, (((0,), (0,)), ((), ())), preferred_element_type=F32)


def _split_bf16(x):
    hi = x.astype(BF16)
    lo = (x - hi.astype(F32)).astype(BF16)
    return hi, lo


def _layer_norm(x, g, b):
    mu = jnp.mean(x, axis=-1, keepdims=True)
    xc = x - mu
    var = jnp.mean(xc * xc, axis=-1, keepdims=True)
    return xc * lax.rsqrt(var + LN_EPS) * g + b


def _softplus(z):
    return jnp.maximum(z, 0.0) + jnp.log(1.0 + jnp.exp(-jnp.abs(z)))


def _later_matrix(n):
    return (lax.broadcasted_iota(jnp.int32, (n, n), 0) > lax.broadcasted_iota(jnp.int32, (n, n), 1)).astype(BF16)


def _params(n_grid):
    return pltpu.CompilerParams(dimension_semantics=("arbitrary",) * n_grid, vmem_limit_bytes=VMEM_LIMIT)


def _const_spec(shape):
    zeros = (0,) * len(shape)
    return pl.BlockSpec(shape, lambda *_: zeros, pipeline_mode=pl.Buffered(1))


def _mod_kernel(c_ref, w_ref, b_ref, o_ref):
    c = c_ref[...]
    a = c / (1.0 + jnp.exp(-c))
    o_ref[...] = _dot(a.astype(BF16), w_ref[...].astype(BF16)) + b_ref[...]


def _mod_call(c_all, w_mod, b_mod, tn=1536):
    r, d = c_all.shape
    n = w_mod.shape[1]
    return pl.pallas_call(
        _mod_kernel,
        out_shape=jax.ShapeDtypeStruct((r, n), F32),
        grid=(n // tn,),
        in_specs=[pl.BlockSpec((r, d), lambda j: (0, 0)),
                  pl.BlockSpec((d, tn), lambda j: (0, j)),
                  pl.BlockSpec((1, tn), lambda j: (0, j))],
        out_specs=pl.BlockSpec((r, tn), lambda j: (0, j)),
        compiler_params=_params(1),
        name="mod",
    )(c_all, w_mod, b_mod)


def _pre_kernel(x_ref, g0_ref, b0_ref, mod_ref, rot_ref, w_ref, qa_ref, qb_ref, ka_ref, va_ref, kb_ref, vb_ref,
                *extra_refs, wa, wb):
    transposed = bool(extra_refs)

    def put(ref, sl, val):
        if transposed:
            ref[sl, :] = val.T
        else:
            ref[:, sl] = val

    xn = _layer_norm(x_ref[...], g0_ref[...], b0_ref[...])
    h = xn * (1.0 + mod_ref[1]) + mod_ref[0]
    qkv = _dot(h.astype(BF16), w_ref[...])
    cos, s_up, s_dn = rot_ref[0], rot_ref[1], rot_ref[2]

    def rotate(t):
        return t * cos + pltpu.roll(t, LANES - ROT_DIM // 2, 1) * s_up + pltpu.roll(t, ROT_DIM // 2, 1) * s_dn

    for t in range(wa // LANES):
        sl = slice(t * LANES, (t + 1) * LANES)
        qa = rotate(qkv[:, sl])
        ka = rotate(qkv[:, wa + t * LANES:wa + (t + 1) * LANES])
        qa_ref[:, sl] = (qa * HEAD_DIM ** -0.5).astype(BF16)
        put(ka_ref, sl, ka)
        if extra_refs:
            kab_ref, km_ref = extra_refs[0], extra_refs[4]
            kab_ref[:, sl] = ka.astype(BF16)
            for g in range(km_ref.shape[0]):
                blk = ka[g * MOBA_BLOCK:(g + 1) * MOBA_BLOCK, :]
                km_ref[g, :, sl] = jnp.sum(blk, axis=0, keepdims=True) * (1.0 / MOBA_BLOCK)
    o = 3 * wa
    va, kb, vb = qkv[:, 2 * wa:o], qkv[:, o + wb:o + 2 * wb], qkv[:, o + 2 * wb:o + 3 * wb]
    qb_ref[...] = (qkv[:, o:o + wb] * HEAD_DIM ** -0.5).astype(BF16)
    for ref, val in ((va_ref, va), (kb_ref, kb), (vb_ref, vb)):
        for t in range(val.shape[1] // LANES):
            sl = slice(t * LANES, (t + 1) * LANES)
            put(ref, sl, val[:, sl])
    if extra_refs:
        extra_refs[1][...] = va.astype(BF16)
        extra_refs[2][...] = kb.astype(BF16)
        extra_refs[3][...] = vb.astype(BF16)


def _pre_call(x2, g0, b0, mods, mod_spec, rot, w_in_bf, wa, wb, tm, prompt):
    r, d = x2.shape
    n_rot = rot.shape[1] // tm
    row = lambda w: pl.BlockSpec((tm, w), lambda i: (i, 0))
    if prompt:
        nt = rot.shape[1] // tm
        kv = lambda w: jax.ShapeDtypeStruct((r // rot.shape[1], w, rot.shape[1]), F32)
        kv_spec = lambda w: pl.BlockSpec((None, w, tm), lambda i: (i // nt, 0, i % nt))
    else:
        kv = lambda w: jax.ShapeDtypeStruct((r, w), F32)
        kv_spec = row
    outs = [jax.ShapeDtypeStruct((r, wa), BF16), jax.ShapeDtypeStruct((r, wb), BF16), kv(wa), kv(wa), kv(wb), kv(wb)]
    out_specs = [row(wa), row(wb), kv_spec(wa), kv_spec(wa), kv_spec(wb), kv_spec(wb)]
    if prompt:
        outs += [jax.ShapeDtypeStruct((r, wa), BF16), jax.ShapeDtypeStruct((r, wa), BF16),
                 jax.ShapeDtypeStruct((r, wb), BF16), jax.ShapeDtypeStruct((r, wb), BF16),
                 jax.ShapeDtypeStruct((r // MOBA_BLOCK, 1, wa), F32)]
        out_specs += [row(wa), row(wa), row(wb), row(wb),
                      pl.BlockSpec((tm // MOBA_BLOCK, 1, wa), lambda i: (i, 0, 0))]
    return pl.pallas_call(
        functools.partial(_pre_kernel, wa=wa, wb=wb),
        out_shape=outs,
        grid=(r // tm,),
        in_specs=[row(d), _const_spec((1, d)), _const_spec((1, d)), mod_spec,
                  pl.BlockSpec((3, tm, LANES), lambda i: (0, i % n_rot, 0)),
                  _const_spec(w_in_bf.shape)],
        out_specs=out_specs,
        compiler_params=_params(1),
        name="pre",
    )(x2, g0, b0, mods, rot, w_in_bf)


def _merge_heads_and_norm(o2, gain, tq):
    lane = lax.broadcasted_iota(jnp.int32, (1, LANES), 1)
    first = lane < HEAD_DIM
    o = jnp.where(first, o2[:tq], o2[tq:])
    sq = o * o
    ss0 = jnp.sum(jnp.where(first, sq, 0.0), axis=1, keepdims=True)
    ss1 = jnp.sum(jnp.where(first, 0.0, sq), axis=1, keepdims=True)
    scale = jnp.where(first, lax.rsqrt(ss0 * (1.0 / HEAD_DIM) + LN_EPS), lax.rsqrt(ss1 * (1.0 / HEAD_DIM) + LN_EPS))
    return o * scale * gain


def _moba_kernel(q_ref, k_ref, v_ref, km_ref, go_ref, o_ref, q2_ref, acc_ref, m_ref, l_ref, *, tq, nblk, n_chain):
    i = pl.program_id(2)
    i_f = i.astype(F32)
    lane = lax.broadcasted_iota(jnp.int32, (1, LANES), 1)
    n_idx = lax.broadcasted_iota(jnp.int32, (nblk, tq), 0).astype(F32)
    valid = n_idx < i_f
    place = (lax.broadcasted_iota(jnp.int32, (nblk, LANES), 1)
             == lax.broadcasted_iota(jnp.int32, (nblk, LANES), 0)).astype(BF16)
    for c in range(n_chain):
        tile = slice(c * LANES, (c + 1) * LANES)
        q = q_ref[:, tile]
        km_hi, km_lo = _split_bf16(km_ref[0, :, tile])
        for hh in range(HEADS_PER_TILE):
            qh = jnp.where((lane // HEAD_DIM) == hh, q, jnp.zeros_like(q))
            g = _dot_nt(km_hi, qh) + _dot_nt(km_lo, qh)
            cand = valid
            sel = n_idx == i_f
            for _ in range(MOBA_TOPK):
                gm = jnp.where(cand, g, -jnp.inf)
                best = jnp.max(gm, axis=0, keepdims=True)
                first = jnp.min(jnp.where(cand & (gm == best), n_idx, float(nblk)), axis=0, keepdims=True)
                pick = n_idx == first
                sel = sel | pick
                cand = cand & jnp.logical_not(pick)
            bias_t = jnp.where(sel, 0.0, NEG).astype(BF16)
            rows = slice(hh * tq, (hh + 1) * tq)
            q2_ref[c, rows, :LANES] = qh
            q2_ref[c, rows, LANES:] = _dot_tn(bias_t, place).astype(BF16)
    row_in_tile = lax.broadcasted_iota(jnp.int32, (2 * tq, 1), 0) & (tq - 1)
    kofs = lax.broadcasted_iota(jnp.int32, (1, MOBA_BLOCK), 1)

    acc_ref[...] = jnp.zeros_like(acc_ref)
    m_ref[...] = jnp.full_like(m_ref, NEG)
    l_ref[...] = jnp.zeros_like(l_ref)

    def step(j, causal):
        start = pl.multiple_of(j * MOBA_BLOCK, MOBA_BLOCK)
        onehot = jnp.broadcast_to((lane == j).astype(BF16), (MOBA_BLOCK, LANES))
        scores = []
        for c in range(n_chain):
            ks = k_ref[pl.ds(start, MOBA_BLOCK), c * LANES:(c + 1) * LANES]
            scores.append(_dot_nt(q2_ref[c], jnp.concatenate([ks, onehot], axis=1)))
        for c in range(n_chain):
            vs = v_ref[pl.ds(start, MOBA_BLOCK), c * LANES:(c + 1) * LANES]
            s = scores[c]
            if causal:
                s = jnp.where(kofs <= row_in_tile, s, NEG)
            m_prev = m_ref[c]
            m_new = jnp.maximum(m_prev, jnp.max(s, axis=1, keepdims=True))
            alpha = jnp.exp(m_prev - m_new)
            p = jnp.exp(s - jnp.concatenate([m_new] * (MOBA_BLOCK // LANES), axis=1))
            l_ref[c] = alpha * l_ref[c] + sum(p[:, u * LANES:(u + 1) * LANES] for u in range(MOBA_BLOCK // LANES))
            acc_ref[c] = alpha * acc_ref[c] + _dot(p.astype(BF16), vs)
            m_ref[c] = m_new

    def past_step(j, carry):
        step(j, False)
        return carry

    lax.fori_loop(0, i, past_step, 0)
    step(i, True)
    for c in range(n_chain):
        tile = slice(c * LANES, (c + 1) * LANES)
        o = acc_ref[c] / jnp.sum(l_ref[c], axis=1, keepdims=True)
        o_ref[:, tile] = _merge_heads_and_norm(o, go_ref[:, tile], tq).astype(o_ref.dtype)


def _moba_call(qa, kab, vab, kmean, g_out, b, s, n_chain):
    tq = MOBA_BLOCK
    nq = s // tq
    nblk = s // MOBA_BLOCK
    w = n_chain * LANES
    n_groups = qa.shape[1] // w
    stats = pltpu.VMEM((n_chain, 2 * tq, LANES), F32)
    return pl.pallas_call(
        functools.partial(_moba_kernel, tq=tq, nblk=nblk, n_chain=n_chain),
        out_shape=jax.ShapeDtypeStruct(qa.shape, BF16),
        grid=(b, n_groups, nq),
        in_specs=[pl.BlockSpec((tq, w), lambda bi, hp, i: (bi * nq + i, hp)),
                  pl.BlockSpec((s, w), lambda bi, hp, i: (bi, hp)),
                  pl.BlockSpec((s, w), lambda bi, hp, i: (bi, hp)),
                  pl.BlockSpec((1, nblk, w), lambda bi, hp, i: (bi, 0, hp)),
                  pl.BlockSpec((1, w), lambda bi, hp, i: (0, hp))],
        out_specs=pl.BlockSpec((tq, w), lambda bi, hp, i: (bi * nq + i, hp)),
        scratch_shapes=[pltpu.VMEM((n_chain, 2 * tq, 2 * LANES), BF16), stats, stats, stats],
        compiler_params=_params(3),
        name="moba",
    )(qa, kab, vab, kmean, g_out)


def _sb_kernel(q_ref, k_ref, v_ref, go_ref, o_ref, q2_ref, acc_ref, car_ref, *, tq, n_chain):
    i = pl.program_id(2)
    lane = lax.broadcasted_iota(jnp.int32, (1, LANES), 1)
    for c in range(n_chain):
        q = q_ref[:, c * LANES:(c + 1) * LANES]
        for hh in range(HEADS_PER_TILE):
            q2_ref[c, hh * tq:(hh + 1) * tq, :] = jnp.where((lane // HEAD_DIM) == hh, q, jnp.zeros_like(q))
    row_in_tile = lax.broadcasted_iota(jnp.int32, (2 * tq, 1), 0) & (tq - 1)
    kofs = lax.broadcasted_iota(jnp.int32, (1, MOBA_BLOCK), 1)
    later = _later_matrix(MOBA_BLOCK)

    def step(j, first):
        start = pl.multiple_of(j * MOBA_BLOCK, MOBA_BLOCK)
        vis = kofs < row_in_tile
        zs = [_dot_nt(q2_ref[c], k_ref[pl.ds(start, MOBA_BLOCK), c * LANES:(c + 1) * LANES]) for c in range(n_chain)]
        mids = []
        for c in range(n_chain):
            sp = _softplus(zs[c])
            spk = jnp.where(vis, sp, 0.0) if first else sp
            mids.append((zs[c] - sp, _dot(spk.astype(BF16), later), jnp.sum(spk, axis=1, keepdims=True)))
        for c in range(n_chain):
            log_beta, after, drop = mids[c]
            w = jnp.exp(log_beta - after)
            if first:
                w = jnp.where(vis, w, 0.0)
            pv = _dot(w.astype(BF16), v_ref[pl.ds(start, MOBA_BLOCK), c * LANES:(c + 1) * LANES])
            if first:
                acc_ref[c] = pv
                car_ref[c] = jnp.broadcast_to(-drop, pv.shape)
            else:
                car = car_ref[c]
                acc_ref[c] += jnp.exp(car) * pv
                car_ref[c] = car - drop

    step(i, True)

    def cond(state):
        t, car_max = state
        return jnp.logical_and(t < i, car_max > UNDERFLOW_LOG)

    def body(state):
        t, _ = state
        step(i - 1 - t, False)
        return t + 1, jnp.max(car_ref[...])

    lax.while_loop(cond, body, (jnp.int32(0), jnp.max(car_ref[...])))
    for c in range(n_chain):
        tile = slice(c * LANES, (c + 1) * LANES)
        o_ref[:, tile] = _merge_heads_and_norm(acc_ref[c], go_ref[:, tile], tq).astype(o_ref.dtype)


def _sb_call(qb, kbb, vbb, g_out, b, s, n_chain):
    tq = MOBA_BLOCK
    nq = s // tq
    w = n_chain * LANES
    n_groups = qb.shape[1] // w
    stats = pltpu.VMEM((n_chain, 2 * tq, LANES), F32)
    return pl.pallas_call(
        functools.partial(_sb_kernel, tq=tq, n_chain=n_chain),
        out_shape=jax.ShapeDtypeStruct(qb.shape, BF16),
        grid=(b, n_groups, nq),
        in_specs=[pl.BlockSpec((tq, w), lambda bi, hp, i: (bi * nq + i, hp)),
                  pl.BlockSpec((s, w), lambda bi, hp, i: (bi, hp)),
                  pl.BlockSpec((s, w), lambda bi, hp, i: (bi, hp)),
                  pl.BlockSpec((1, w), lambda bi, hp, i: (0, hp))],
        out_specs=pl.BlockSpec((tq, w), lambda bi, hp, i: (bi * nq + i, hp)),
        scratch_shapes=[pltpu.VMEM((n_chain, 2 * tq, LANES), BF16), stats, stats],
        compiler_params=_params(3),
        name="sb",
    )(qb, kbb, vbb, g_out)


def _block_diag_queries(q, n_heads):
    t, w = q.shape
    head_of_lane = lax.broadcasted_iota(jnp.int32, (n_heads, w), 1) // HEAD_DIM
    hmask = head_of_lane == lax.broadcasted_iota(jnp.int32, (n_heads, w), 0)
    rows = [jnp.where(hmask, jnp.broadcast_to(q[ti:ti + 1, :], (n_heads, w)), 0.0) for ti in range(t)]
    return jnp.concatenate(rows, axis=0), jnp.concatenate([hmask] * t, axis=0)


def _collapse_heads(o, hmask_rows, gain, n_tok, n_heads):
    oh = jnp.where(hmask_rows, o, 0.0)
    ss = jnp.sum(oh * oh, axis=1, keepdims=True) * (1.0 / HEAD_DIM)
    of = oh * lax.rsqrt(ss + LN_EPS)
    rows = [jnp.sum(of[ti * n_heads:(ti + 1) * n_heads, :], axis=0, keepdims=True) for ti in range(n_tok)]
    return jnp.concatenate(rows, axis=0) * gain


def _smoba_kernel(pt_ref, q_ref, kn_ref, vn_ref, go_ref, *rest, n_pages, n_tok, n_heads):
    del pt_ref
    k_pages, v_pages, o_ref = rest[:n_pages], rest[n_pages:2 * n_pages], rest[2 * n_pages]
    qbd, hmask_rows = _block_diag_queries(q_ref[0].astype(F32), n_heads)
    qbd_bf = qbd.astype(BF16)
    kn, vn = kn_ref[0], vn_ref[0]
    tok_of_row = lax.broadcasted_iota(jnp.int32, (n_tok * n_heads, 1), 0) // n_heads
    pages_per_block = MOBA_BLOCK // PAGE_SIZE
    n_blocks = n_pages // pages_per_block
    s_pages = [_dot(qbd_bf, k_pages[p][...].astype(BF16)) for p in range(n_pages)]
    gates = [sum(jnp.sum(s_pages[n * pages_per_block + u], axis=1, keepdims=True) for u in range(pages_per_block))
             for n in range(n_blocks)]
    biases = []
    for n in range(n_blocks):
        rank = jnp.zeros_like(gates[n])
        for n2 in range(n_blocks):
            if n2 != n:
                beats = (gates[n2] > gates[n]) | ((gates[n2] == gates[n]) & (n2 < n))
                rank = rank + beats.astype(F32)
        biases.append(jnp.where(rank < min(MOBA_TOPK, n_blocks), 0.0, NEG))
    s_new = [jnp.where(t <= tok_of_row, jnp.sum(qbd * kn[t:t + 1, :], axis=1, keepdims=True), NEG)
             for t in range(n_tok)]
    s_sel = [s_pages[p] + biases[p // pages_per_block] for p in range(n_pages)]
    m = functools.reduce(jnp.maximum, s_new)
    for p in range(n_pages):
        m = jnp.maximum(m, jnp.max(s_sel[p], axis=1, keepdims=True))
    acc = jnp.zeros(qbd.shape, F32)
    l = jnp.zeros_like(m)
    for t in range(n_tok):
        pt = jnp.exp(s_new[t] - m)
        l = l + pt
        acc = acc + pt * vn[t:t + 1, :]
    for p in range(n_pages):
        pp = jnp.exp(s_sel[p] - m)
        l = l + jnp.sum(pp, axis=1, keepdims=True)
        acc = acc + _dot_nt(pp.astype(BF16), v_pages[p][...].astype(BF16))
    o_ref[0] = _collapse_heads(acc / l, hmask_rows, go_ref[...], n_tok, n_heads).astype(o_ref.dtype)


def _sb_new_tokens(qbd, kn, vn, n_tok, n_heads):
    tok_of_row = lax.broadcasted_iota(jnp.int32, (n_tok * n_heads, 1), 0) // n_heads
    acc = jnp.zeros(qbd.shape, F32)
    carry = jnp.zeros((n_tok * n_heads, 1), F32)
    for t in reversed(range(n_tok)):
        z = jnp.sum(qbd * kn[t:t + 1, :], axis=1, keepdims=True)
        sp = _softplus(z)
        vis = t < tok_of_row
        acc = acc + jnp.where(vis, jnp.exp(z - sp + carry), 0.0) * vn[t:t + 1, :]
        carry = carry - jnp.where(vis, sp, 0.0)
    return acc, carry


def _ssb_recent_kernel(pt_ref, q_ref, kn_ref, vn_ref, go_ref, *rest, n_recent, group, n_tok, n_heads):
    del pt_ref
    n_in = group * n_recent
    k_pages, v_pages = rest[:n_in], rest[n_in:2 * n_in]
    o_ref, car_ref = rest[2 * n_in:]
    later = _later_matrix(n_recent * PAGE_SIZE)
    gather = lambda pages, g: jnp.concatenate([pages[g * n_recent + u][...] for u in range(n_recent)],
                                              axis=1).astype(BF16)
    state = []
    for g in range(group):
        qbd, hmask_rows = _block_diag_queries(q_ref[g].astype(F32), n_heads)
        acc, carry = _sb_new_tokens(qbd, kn_ref[g], vn_ref[g], n_tok, n_heads)
        state.append((hmask_rows, acc, carry, _dot(qbd.astype(BF16), gather(k_pages, g))))
    mids = []
    for hmask_rows, acc, carry, z in state:
        sp = _softplus(z)
        mids.append((z - sp, _dot(sp.astype(BF16), later), jnp.sum(sp, axis=1, keepdims=True)))
    for g in range(group):
        hmask_rows, acc, carry, _ = state[g]
        log_beta, after, drop = mids[g]
        w = jnp.exp(log_beta - after)
        acc = acc + jnp.exp(carry) * _dot_nt(w.astype(BF16), gather(v_pages, g))
        o_ref[g] = _collapse_heads(acc, hmask_rows, go_ref[...], n_tok, n_heads).astype(o_ref.dtype)
        car_ref[g] = jnp.broadcast_to(jnp.max(carry - drop, axis=0, keepdims=True), car_ref.shape[1:])


def _ssb_recent_call(page_table, q, k_new, v_new, g_out, cache_k, cache_v, n_recent, group):
    db, n_pages = page_table.shape
    _, t, w = q.shape
    tok = lambda: pl.BlockSpec((group, t, w), lambda bi, pt: (bi, 0, 0))

    def page(g, u):
        return pl.BlockSpec((None, w, PAGE_SIZE),
                            lambda bi, pt: (pt[bi * group + g, n_pages - n_recent + u], 0, 0))

    pages = [page(g, u) for g in range(group) for u in range(n_recent)]
    grid_spec = pltpu.PrefetchScalarGridSpec(
        num_scalar_prefetch=1,
        grid=(db // group,),
        in_specs=[tok(), tok(), tok(), pl.BlockSpec((1, w), lambda bi, pt: (0, 0))] + pages * 2,
        out_specs=[tok(), pl.BlockSpec((group, 1, LANES), lambda bi, pt: (bi, 0, 0))],
    )
    return pl.pallas_call(
        functools.partial(_ssb_recent_kernel, n_recent=n_recent, group=group, n_tok=t, n_heads=w // HEAD_DIM),
        out_shape=[jax.ShapeDtypeStruct(q.shape, BF16), jax.ShapeDtypeStruct((db, 1, LANES), F32)],
        grid_spec=grid_spec,
        compiler_params=_params(1),
        name="ssb_recent",
    )(page_table, q, k_new, v_new, g_out, *([cache_k] * len(pages)), *([cache_v] * len(pages)))


def _ssb_kernel(pt_ref, q_ref, kn_ref, vn_ref, go_ref, *rest, n_pages, n_tok, n_heads):
    del pt_ref
    k_pages, v_pages = rest[:n_pages], rest[n_pages:2 * n_pages]
    o_ref, acc_ref, car_ref = rest[2 * n_pages:]
    qbd, hmask_rows = _block_diag_queries(q_ref[0].astype(F32), n_heads)
    qbd_bf = qbd.astype(BF16)
    later = _later_matrix(PAGE_SIZE)
    acc, carry = _sb_new_tokens(qbd, kn_ref[0], vn_ref[0], n_tok, n_heads)
    acc_ref[...] = acc
    car_ref[...] = jnp.broadcast_to(carry, car_ref.shape)
    for p in reversed(range(n_pages)):
        @pl.when(jnp.max(car_ref[...]) > UNDERFLOW_LOG)
        def _():
            z = _dot(qbd_bf, k_pages[p][...].astype(BF16))
            sp = _softplus(z)
            w = jnp.exp(z - sp - _dot(sp.astype(BF16), later))
            car = car_ref[...]
            acc_ref[...] += jnp.exp(car[:, :1]) * _dot_nt(w.astype(BF16), v_pages[p][...].astype(BF16))
            car_ref[...] = car - jnp.sum(sp, axis=1, keepdims=True)
    o_ref[0] = _collapse_heads(acc_ref[...], hmask_rows, go_ref[...], n_tok, n_heads).astype(o_ref.dtype)


def _sample_attn_call(body, name, page_table, q, k_new, v_new, g_out, cache_k, cache_v, scratch_shapes=()):
    db, n_pages = page_table.shape
    _, t, w = q.shape
    n_heads = w // HEAD_DIM
    tok = lambda: pl.BlockSpec((1, t, w), lambda bi, pt: (bi, 0, 0))
    page = lambda p: pl.BlockSpec((None, w, PAGE_SIZE), lambda bi, pt: (pt[bi, p], 0, 0))
    grid_spec = pltpu.PrefetchScalarGridSpec(
        num_scalar_prefetch=1,
        grid=(db,),
        in_specs=[tok(), tok(), tok(), pl.BlockSpec((1, w), lambda bi, pt: (0, 0))]
                 + [page(p) for p in range(n_pages)] * 2,
        out_specs=tok(),
        scratch_shapes=scratch_shapes,
    )
    return pl.pallas_call(
        functools.partial(body, n_pages=n_pages, n_tok=t, n_heads=n_heads),
        out_shape=jax.ShapeDtypeStruct(q.shape, BF16),
        grid_spec=grid_spec,
        compiler_params=_params(1),
        name=name,
    )(page_table, q, k_new, v_new, g_out, *([cache_k] * n_pages), *([cache_v] * n_pages))


def _post_kernel(x_ref, g0_ref, b0_ref, ma_ref, mb_ref, mod_ref, wo_ref, g1_ref, b1_ref,
                 wg_ref, wu_ref, wd_ref, g2_ref, b2_ref, o_ref, *, alpha, wa, ff_chunk):
    xn = _layer_norm(x_ref[...], g0_ref[...], b0_ref[...])
    attn = _dot(ma_ref[...], wo_ref[:wa, :]) + _dot(mb_ref[...], wo_ref[wa:, :])
    x1 = _layer_norm(alpha * xn + mod_ref[2] * attn, g1_ref[...], b1_ref[...])
    hb = (x1 * (1.0 + mod_ref[4]) + mod_ref[3]).astype(BF16)
    f = jnp.zeros(x1.shape, F32)
    for c in range(0, wg_ref.shape[1], ff_chunk):
        gate = _dot(hb, wg_ref[:, c:c + ff_chunk])
        up = _dot(hb, wu_ref[:, c:c + ff_chunk])
        act = gate / (1.0 + jnp.exp(-gate)) * up
        f = f + _dot(act.astype(BF16), wd_ref[c:c + ff_chunk, :])
    o_ref[...] = _layer_norm(alpha * x1 + mod_ref[5] * f, g2_ref[...], b2_ref[...])


def _post_call(x2, g0, b0, ma, mb, mods, mod_spec, wo, g1, b1, wg, wu, wd, g2, b2, alpha, tm, ff_chunk):
    r, d = x2.shape
    wa = ma.shape[1]
    row = lambda w: pl.BlockSpec((tm, w), lambda i: (i, 0))
    vec = lambda: _const_spec((1, d))
    return pl.pallas_call(
        functools.partial(_post_kernel, alpha=alpha, wa=wa, ff_chunk=ff_chunk),
        out_shape=jax.ShapeDtypeStruct((r, d), F32),
        grid=(r // tm,),
        in_specs=[row(d), vec(), vec(), row(wa), row(mb.shape[1]), mod_spec, _const_spec(wo.shape), vec(), vec(),
                  _const_spec(wg.shape), _const_spec(wu.shape), _const_spec(wd.shape), vec(), vec()],
        out_specs=row(d),
        compiler_params=_params(1),
        name="post",
    )(x2, g0, b0, ma, mb, mods, wo, g1, b1, wg, wu, wd, g2, b2)


def _rotary_tables(pos):
    half = ROT_DIM // 2
    inv = ROPE_THETA ** (-jnp.arange(half, dtype=F32) * 2.0 / ROT_DIM)
    ang = pos.astype(F32)[:, None] * inv
    cos, sin = jnp.cos(ang), jnp.sin(ang)
    n = pos.shape[0]
    rest = HEAD_DIM - ROT_DIM
    c = jnp.concatenate([cos, cos, jnp.ones((n, rest), F32)], axis=1)
    s_up = jnp.concatenate([-sin, jnp.zeros((n, half + rest), F32)], axis=1)
    s_dn = jnp.concatenate([jnp.zeros((n, half), F32), sin, jnp.zeros((n, rest), F32)], axis=1)
    return jnp.stack([jnp.tile(t, (1, HEADS_PER_TILE)) for t in (c, s_up, s_dn)])


def kernel(x_prompt, x_sample, cache_moba_k, cache_moba_v, cache_sb_k, cache_sb_v, page_table, c_prompt, c_sample,
           ln0_g, ln0_b, w_mod, b_mod, w_in, g_out_a, g_out_b, w_out, ln1_g, ln1_b, w_gate, w_up, w_down,
           ln2_g, ln2_b):
    b, s, d = x_prompt.shape
    db, t, _ = x_sample.shape
    depth = w_mod.shape[0]
    assert depth == 1, "ln0 is fused into the first layer's kernels; only a one-layer trunk is supported"
    wa, wb = g_out_a.shape[1], g_out_b.shape[1]
    ha, hb = wa // HEAD_DIM, wb // HEAD_DIM
    n_pages = page_table.shape[1]
    past = n_pages * PAGE_SIZE
    assert s % MOBA_BLOCK == 0 and past % MOBA_BLOCK == 0
    alpha = (2 * depth) ** 0.25
    vec = lambda a: a.reshape(1, -1)

    n_mod = 6
    rows = b + db
    pad = -rows % 8
    c_all = jnp.concatenate([c_prompt, c_sample, jnp.zeros((pad, d), F32)], axis=0)
    m = _mod_call(c_all, w_mod[0], b_mod[0].reshape(1, -1))
    mods_p = m[:b].reshape(b, n_mod, 1, d)
    mods_s = jnp.repeat(m[b:rows].reshape(db, n_mod, d), t, axis=0).transpose(1, 0, 2)

    w_in_bf = w_in[0].astype(BF16)
    wo_bf, wg_bf, wu_bf, wd_bf = (w[0].astype(BF16) for w in (w_out, w_gate, w_up, w_down))
    g0, b0 = vec(ln0_g), vec(ln0_b)
    ln = [vec(a[0]) for a in (ln1_g, ln1_b, ln2_g, ln2_b)]
    goa, gob = g_out_a[0].reshape(1, wa), g_out_b[0].reshape(1, wb)

    tm = 512
    ff_chunk = 1408

    xp2 = x_prompt.reshape(b * s, d)
    mod_spec_p = pl.BlockSpec((None, n_mod, 1, d), lambda i: (i // (s // tm), 0, 0, 0))
    rot_p = _rotary_tables(jnp.arange(s, dtype=jnp.int32))
    (qa, qb, ka, va, kb, vb, kab, vab, kbb, vbb, kmean) = _pre_call(
        xp2, g0, b0, mods_p, mod_spec_p, rot_p, w_in_bf, wa, wb, tm, True)
    mixed_a = _moba_call(qa, kab, vab, kmean.reshape(b, s // MOBA_BLOCK, wa), goa, b, s, n_chain=4)
    mixed_b = _sb_call(qb, kbb, vbb, gob, b, s, n_chain=4)
    y_p = _post_call(xp2, g0, b0, mixed_a, mixed_b, mods_p, mod_spec_p, wo_bf, ln[0], ln[1],
                     wg_bf, wu_bf, wd_bf, ln[2], ln[3], alpha, tm, ff_chunk)

    rs = db * t
    xs2 = x_sample.reshape(rs, d)
    mod_spec_s = pl.BlockSpec((n_mod, rs, d), lambda i: (0, 0, 0))
    rot_s = _rotary_tables(jnp.tile(past + jnp.arange(t, dtype=jnp.int32), db))
    (sqa, sqb, ska, sva, skb, svb) = _pre_call(
        xs2, g0, b0, mods_s, mod_spec_s, rot_s, w_in_bf, wa, wb, rs, False)
    tok3 = lambda a: a.reshape(db, t, a.shape[-1])
    pages = lambda c: c[0].transpose(0, 2, 3, 1).reshape(c.shape[1], -1, PAGE_SIZE)
    smixed_a = _sample_attn_call(_smoba_kernel, "smoba", page_table, tok3(sqa), tok3(ska), tok3(sva), goa,
                                 pages(cache_moba_k), pages(cache_moba_v))
    sb_args = (page_table, tok3(sqb), tok3(skb), tok3(svb), gob, pages(cache_sb_k), pages(cache_sb_v))
    recent_b, log_keep = _ssb_recent_call(*sb_args, n_recent=min(SB_RECENT_PAGES, n_pages), group=SB_GROUP)
    smixed_b = lax.cond(
        jnp.max(log_keep) > UNDERFLOW_LOG,
        lambda: _sample_attn_call(_ssb_kernel, "ssb", *sb_args, scratch_shapes=[
            pltpu.VMEM((t * hb, wb), F32), pltpu.VMEM((t * hb, LANES), F32)]),
        lambda: recent_b)
    y_s = _post_call(xs2, g0, b0, smixed_a.reshape(rs, wa), smixed_b.reshape(rs, wb), mods_s, mod_spec_s,
                     wo_bf, ln[0], ln[1], wg_bf, wu_bf, wd_bf, ln[2], ln[3], alpha, rs, ff_chunk)

    kv_p = lambda a, h: a.reshape(b, h, HEAD_DIM, s).transpose(0, 3, 1, 2)[None]
    return (y_p.reshape(b, s, d), y_s.reshape(db, t, d),
            kv_p(ka, ha), kv_p(va, ha), kv_p(kb, hb), kv_p(vb, hb),
            ska.reshape(1, db, t, ha, HEAD_DIM), sva.reshape(1, db, t, ha, HEAD_DIM),
            skb.reshape(1, db, t, hb, HEAD_DIM), svb.reshape(1, db, t, hb, HEAD_DIM))
```

```python
import functools

import jax
import jax.numpy as jnp
from jax import lax
from jax.experimental import pallas as pl
from jax.experimental.pallas import tpu as pltpu

F32 = jnp.float32
BF16 = jnp.bfloat16

HEAD_DIM = 64
MOBA_BLOCK = 256
MOBA_TOPK = 3
PAGE_SIZE = 128
ROPE_THETA = 500000.0
ROT_DIM = HEAD_DIM // 4
LN_EPS = 1e-5
LANES = 128
HEADS_PER_TILE = LANES // HEAD_DIM
NEG = -1e30
UNDERFLOW_LOG = -110.0
MOBA_SCORE_LEAD = 4
SB_RECENT_PAGES = 2
SB_GROUP = 4
VMEM_LIMIT = 56 * 1024 * 1024


def _dot(a, b):
    return jnp.dot(a, b, preferred_element_type=F32)


def _dot_nt(a, b):
    return lax.dot_general(a, b, (((1,), (1,)), ((), ())), preferred_element_type=F32)


def _dot_tn(a, b):
    return lax.dot_general(a, b, (((0,), (0,)), ((), ())), preferred_element_type=F32)


def _split_bf16(x):
    hi = x.astype(BF16)
    lo = (x - hi.astype(F32)).astype(BF16)
    return hi, lo


def _layer_norm(x, g, b):
    mu = jnp.mean(x, axis=-1, keepdims=True)
    xc = x - mu
    var = jnp.mean(xc * xc, axis=-1, keepdims=True)
    return xc * lax.rsqrt(var + LN_EPS) * g + b


def _softplus(z):
    return jnp.maximum(z, 0.0) + jnp.log(1.0 + jnp.exp(-jnp.abs(z)))


def _later_matrix(n):
    return (lax.broadcasted_iota(jnp.int32, (n, n), 0) > lax.broadcasted_iota(jnp.int32, (n, n), 1)).astype(BF16)


def _params(n_grid):
    return pltpu.CompilerParams(dimension_semantics=("arbitrary",) * n_grid, vmem_limit_bytes=VMEM_LIMIT)


def _const_spec(shape):
    zeros = (0,) * len(shape)
    return pl.BlockSpec(shape, lambda *_: zeros, pipeline_mode=pl.Buffered(1))


def _mod_kernel(c_ref, w_ref, b_ref, o_ref):
    c = c_ref[...]
    a = c / (1.0 + jnp.exp(-c))
    o_ref[...] = _dot(a.astype(BF16), w_ref[...].astype(BF16)) + b_ref[...]


def _mod_call(c_all, w_mod, b_mod, tn=1536):
    r, d = c_all.shape
    n = w_mod.shape[1]
    return pl.pallas_call(
        _mod_kernel,
        out_shape=jax.ShapeDtypeStruct((r, n), F32),
        grid=(n // tn,),
        in_specs=[pl.BlockSpec((r, d), lambda j: (0, 0)),
                  pl.BlockSpec((d, tn), lambda j: (0, j)),
                  pl.BlockSpec((1, tn), lambda j: (0, j))],
        out_specs=pl.BlockSpec((r, tn), lambda j: (0, j)),
        compiler_params=_params(1),
        name="mod",
    )(c_all, w_mod, b_mod)


def _pre_kernel(x_ref, g0_ref, b0_ref, mod_ref, rot_ref, w_ref, qa_ref, qb_ref, ka_ref, va_ref, kb_ref, vb_ref,
                *extra_refs, wa, wb):
    transposed = bool(extra_refs)

    def put(ref, sl, val):
        if transposed:
            ref[sl, :] = val.T
        else:
            ref[:, sl] = val

    xn = _layer_norm(x_ref[...], g0_ref[...], b0_ref[...])
    h = xn * (1.0 + mod_ref[1]) + mod_ref[0]
    qkv = _dot(h.astype(BF16), w_ref[...])
    cos, s_up, s_dn = rot_ref[0], rot_ref[1], rot_ref[2]

    def rotate(t):
        return t * cos + pltpu.roll(t, LANES - ROT_DIM // 2, 1) * s_up + pltpu.roll(t, ROT_DIM // 2, 1) * s_dn

    for t in range(wa // LANES):
        sl = slice(t * LANES, (t + 1) * LANES)
        qa = rotate(qkv[:, sl])
        ka = rotate(qkv[:, wa + t * LANES:wa + (t + 1) * LANES])
        qa_ref[:, sl] = (qa * HEAD_DIM ** -0.5).astype(BF16)
        put(ka_ref, sl, ka)
        if extra_refs:
            kab_ref, km_ref = extra_refs[0], extra_refs[4]
            kab_ref[:, sl] = ka.astype(BF16)
            for g in range(km_ref.shape[0]):
                blk = ka[g * MOBA_BLOCK:(g + 1) * MOBA_BLOCK, :]
                km_ref[g, :, sl] = jnp.sum(blk, axis=0, keepdims=True) * (1.0 / MOBA_BLOCK)
    o = 3 * wa
    va, kb, vb = qkv[:, 2 * wa:o], qkv[:, o + wb:o + 2 * wb], qkv[:, o + 2 * wb:o + 3 * wb]
    qb_ref[...] = (qkv[:, o:o + wb] * HEAD_DIM ** -0.5).astype(BF16)
    for ref, val in ((va_ref, va), (kb_ref, kb), (vb_ref, vb)):
        for t in range(val.shape[1] // LANES):
            sl = slice(t * LANES, (t + 1) * LANES)
            put(ref, sl, val[:, sl])
    if extra_refs:
        extra_refs[1][...] = va.astype(BF16)
        extra_refs[2][...] = kb.astype(BF16)
        extra_refs[3][...] = vb.astype(BF16)


def _pre_call(x2, g0, b0, mods, mod_spec, rot, w_in_bf, wa, wb, tm, prompt):
    r, d = x2.shape
    n_rot = rot.shape[1] // tm
    row = lambda w: pl.BlockSpec((tm, w), lambda i: (i, 0))
    if prompt:
        nt = rot.shape[1] // tm
        kv = lambda w: jax.ShapeDtypeStruct((r // rot.shape[1], w, rot.shape[1]), F32)
        kv_spec = lambda w: pl.BlockSpec((None, w, tm), lambda i: (i // nt, 0, i % nt))
    else:
        kv = lambda w: jax.ShapeDtypeStruct((r, w), F32)
        kv_spec = row
    outs = [jax.ShapeDtypeStruct((r, wa), BF16), jax.ShapeDtypeStruct((r, wb), BF16), kv(wa), kv(wa), kv(wb), kv(wb)]
    out_specs = [row(wa), row(wb), kv_spec(wa), kv_spec(wa), kv_spec(wb), kv_spec(wb)]
    if prompt:
        outs += [jax.ShapeDtypeStruct((r, wa), BF16), jax.ShapeDtypeStruct((r, wa), BF16),
                 jax.ShapeDtypeStruct((r, wb), BF16), jax.ShapeDtypeStruct((r, wb), BF16),
                 jax.ShapeDtypeStruct((r // MOBA_BLOCK, 1, wa), F32)]
        out_specs += [row(wa), row(wa), row(wb), row(wb),
                      pl.BlockSpec((tm // MOBA_BLOCK, 1, wa), lambda i: (i, 0, 0))]
    return pl.pallas_call(
        functools.partial(_pre_kernel, wa=wa, wb=wb),
        out_shape=outs,
        grid=(r // tm,),
        in_specs=[row(d), _const_spec((1, d)), _const_spec((1, d)), mod_spec,
                  pl.BlockSpec((3, tm, LANES), lambda i: (0, i % n_rot, 0)),
                  _const_spec(w_in_bf.shape)],
        out_specs=out_specs,
        compiler_params=_params(1),
        name="pre",
    )(x2, g0, b0, mods, rot, w_in_bf)


def _merge_heads_and_norm(o2, gain, tq):
    lane = lax.broadcasted_iota(jnp.int32, (1, LANES), 1)
    first = lane < HEAD_DIM
    o = jnp.where(first, o2[:tq], o2[tq:])
    sq = o * o
    ss0 = jnp.sum(jnp.where(first, sq, 0.0), axis=1, keepdims=True)
    ss1 = jnp.sum(jnp.where(first, 0.0, sq), axis=1, keepdims=True)
    scale = jnp.where(first, lax.rsqrt(ss0 * (1.0 / HEAD_DIM) + LN_EPS), lax.rsqrt(ss1 * (1.0 / HEAD_DIM) + LN_EPS))
    return o * scale * gain


def _moba_kernel(q_ref, k_ref, v_ref, km_ref, go_ref, o_ref, q2_ref, s_ref, acc_ref, m_ref, *,
                 tq, nblk, n_chain):
    i = pl.program_id(2)
    i_f = i.astype(F32)
    lane = lax.broadcasted_iota(jnp.int32, (1, LANES), 1)
    n_idx = lax.broadcasted_iota(jnp.int32, (nblk, tq), 0).astype(F32)
    valid = n_idx < i_f
    place = (lax.broadcasted_iota(jnp.int32, (nblk, LANES), 1)
             == lax.broadcasted_iota(jnp.int32, (nblk, LANES), 0)).astype(BF16)
    for c in range(n_chain):
        tile = slice(c * LANES, (c + 1) * LANES)
        q = q_ref[:, tile]
        km_hi, km_lo = _split_bf16(km_ref[0, :, tile])
        for hh in range(HEADS_PER_TILE):
            qh = jnp.where((lane // HEAD_DIM) == hh, q, jnp.zeros_like(q))
            g = _dot_nt(km_hi, qh) + _dot_nt(km_lo, qh)
            cand = valid
            sel = n_idx == i_f
            for _ in range(MOBA_TOPK):
                gm = jnp.where(cand, g, -jnp.inf)
                best = jnp.max(gm, axis=0, keepdims=True)
                first = jnp.min(jnp.where(cand & (gm == best), n_idx, float(nblk)), axis=0, keepdims=True)
                pick = n_idx == first
                sel = sel | pick
                cand = cand & jnp.logical_not(pick)
            bias_t = jnp.where(sel, 0.0, NEG).astype(BF16)
            rows = slice(hh * tq, (hh + 1) * tq)
            q2_ref[c, rows, :LANES] = qh
            q2_ref[c, rows, LANES:] = _dot_tn(bias_t, place).astype(BF16)
    row_in_tile = lax.broadcasted_iota(jnp.int32, (2 * tq, 1), 0) & (tq - 1)
    kofs = lax.broadcasted_iota(jnp.int32, (1, MOBA_BLOCK), 1)

    first_head = lane < HEAD_DIM

    acc_ref[...] = jnp.zeros_like(acc_ref)
    m_ref[...] = jnp.full_like(m_ref, NEG)

    def score(c, j):
        ks = k_ref[pl.ds(pl.multiple_of(j * MOBA_BLOCK, MOBA_BLOCK), MOBA_BLOCK), c * LANES:(c + 1) * LANES]
        onehot = jnp.broadcast_to((lane == j).astype(BF16), (MOBA_BLOCK, LANES))
        return _dot_nt(q2_ref[c], jnp.concatenate([ks, onehot], axis=1))

    lead = s_ref.shape[0]
    for c in range(lead):
        s_ref[c] = score(c, 0)

    def step(j, causal):
        start = pl.multiple_of(j * MOBA_BLOCK, MOBA_BLOCK)
        scores = [s_ref[c] for c in range(lead)]
        for c in range(n_chain):
            if c + lead < n_chain:
                scores.append(score(c + lead, j))
            elif not causal:
                s_ref[c + lead - n_chain] = score(c + lead - n_chain, j + 1)
            vs = v_ref[pl.ds(start, MOBA_BLOCK), c * LANES:(c + 1) * LANES]
            s = scores[c]
            if causal:
                s = jnp.where(kofs <= row_in_tile, s, NEG)
            m_prev = m_ref[c]
            m_new = jnp.maximum(m_prev, jnp.max(s, axis=1, keepdims=True))
            alpha = jnp.exp(m_prev - m_new)
            p = jnp.exp(s - jnp.concatenate([m_new] * (MOBA_BLOCK // LANES), axis=1)).astype(BF16)
            ones = jnp.ones_like(vs)
            pv = [_dot(p[:tq], jnp.where(first_head, vs, ones)), _dot(p[tq:], jnp.where(first_head, ones, vs))]
            acc_ref[c] = alpha * acc_ref[c] + jnp.concatenate(pv, axis=0)
            m_ref[c] = m_new

    def past_step(j, carry):
        step(j, False)
        return carry

    lax.fori_loop(0, i, past_step, 0)
    step(i, True)
    for c in range(n_chain):
        tile = slice(c * LANES, (c + 1) * LANES)
        acc = acc_ref[c]
        den = jnp.concatenate([jnp.max(jnp.where(first_head, 0.0, acc[:tq]), axis=1, keepdims=True),
                               jnp.max(jnp.where(first_head, acc[tq:], 0.0), axis=1, keepdims=True)], axis=0)
        o_ref[:, tile] = _merge_heads_and_norm(acc / den, go_ref[:, tile], tq).astype(o_ref.dtype)


def _moba_call(qa, kab, vab, kmean, g_out, b, s, n_chain):
    tq = MOBA_BLOCK
    nq = s // tq
    nblk = s // MOBA_BLOCK
    w = n_chain * LANES
    n_groups = qa.shape[1] // w
    stats = pltpu.VMEM((n_chain, 2 * tq, LANES), F32)
    return pl.pallas_call(
        functools.partial(_moba_kernel, tq=tq, nblk=nblk, n_chain=n_chain),
        out_shape=jax.ShapeDtypeStruct(qa.shape, BF16),
        grid=(b, n_groups, nq),
        in_specs=[pl.BlockSpec((tq, w), lambda bi, hp, i: (bi * nq + i, hp)),
                  pl.BlockSpec((s, w), lambda bi, hp, i: (bi, hp)),
                  pl.BlockSpec((s, w), lambda bi, hp, i: (bi, hp)),
                  pl.BlockSpec((1, nblk, w), lambda bi, hp, i: (bi, 0, hp)),
                  pl.BlockSpec((1, w), lambda bi, hp, i: (0, hp))],
        out_specs=pl.BlockSpec((tq, w), lambda bi, hp, i: (bi * nq + i, hp)),
        scratch_shapes=[pltpu.VMEM((n_chain, 2 * tq, 2 * LANES), BF16),
                        pltpu.VMEM((MOBA_SCORE_LEAD, 2 * tq, MOBA_BLOCK), F32), stats, stats],
        compiler_params=_params(3),
        name="moba",
    )(qa, kab, vab, kmean, g_out)


def _sb_kernel(q_ref, k_ref, v_ref, go_ref, o_ref, q2_ref, acc_ref, car_ref, *, tq, n_chain):
    i = pl.program_id(2)
    lane = lax.broadcasted_iota(jnp.int32, (1, LANES), 1)
    for c in range(n_chain):
        q = q_ref[:, c * LANES:(c + 1) * LANES]
        for hh in range(HEADS_PER_TILE):
            q2_ref[c, hh * tq:(hh + 1) * tq, :] = jnp.where((lane // HEAD_DIM) == hh, q, jnp.zeros_like(q))
    row_in_tile = lax.broadcasted_iota(jnp.int32, (2 * tq, 1), 0) & (tq - 1)
    kofs = lax.broadcasted_iota(jnp.int32, (1, MOBA_BLOCK), 1)
    later = _later_matrix(MOBA_BLOCK)

    def step(j, first):
        start = pl.multiple_of(j * MOBA_BLOCK, MOBA_BLOCK)
        vis = kofs < row_in_tile
        zs = [_dot_nt(q2_ref[c], k_ref[pl.ds(start, MOBA_BLOCK), c * LANES:(c + 1) * LANES]) for c in range(n_chain)]
        mids = []
        for c in range(n_chain):
            sp = _softplus(zs[c])
            spk = jnp.where(vis, sp, 0.0) if first else sp
            mids.append((zs[c] - sp, _dot(spk.astype(BF16), later), jnp.sum(spk, axis=1, keepdims=True)))
        for c in range(n_chain):
            log_beta, after, drop = mids[c]
            w = jnp.exp(log_beta - after)
            if first:
                w = jnp.where(vis, w, 0.0)
            pv = _dot(w.astype(BF16), v_ref[pl.ds(start, MOBA_BLOCK), c * LANES:(c + 1) * LANES])
            if first:
                acc_ref[c] = pv
                car_ref[c] = jnp.broadcast_to(-drop, pv.shape)
            else:
                car = car_ref[c]
                acc_ref[c] += jnp.exp(car) * pv
                car_ref[c] = car - drop

    step(i, True)

    def cond(state):
        t, car_max = state
        return jnp.logical_and(t < i, car_max > UNDERFLOW_LOG)

    def body(state):
        t, _ = state
        step(i - 1 - t, False)
        return t + 1, jnp.max(car_ref[...])

    lax.while_loop(cond, body, (jnp.int32(0), jnp.max(car_ref[...])))
    for c in range(n_chain):
        tile = slice(c * LANES, (c + 1) * LANES)
        o_ref[:, tile] = _merge_heads_and_norm(acc_ref[c], go_ref[:, tile], tq).astype(o_ref.dtype)


def _sb_call(qb, kbb, vbb, g_out, b, s, n_chain):
    tq = MOBA_BLOCK
    nq = s // tq
    w = n_chain * LANES
    n_groups = qb.shape[1] // w
    stats = pltpu.VMEM((n_chain, 2 * tq, LANES), F32)
    return pl.pallas_call(
        functools.partial(_sb_kernel, tq=tq, n_chain=n_chain),
        out_shape=jax.ShapeDtypeStruct(qb.shape, BF16),
        grid=(b, n_groups, nq),
        in_specs=[pl.BlockSpec((tq, w), lambda bi, hp, i: (bi * nq + i, hp)),
                  pl.BlockSpec((s, w), lambda bi, hp, i: (bi, hp)),
                  pl.BlockSpec((s, w), lambda bi, hp, i: (bi, hp)),
                  pl.BlockSpec((1, w), lambda bi, hp, i: (0, hp))],
        out_specs=pl.BlockSpec((tq, w), lambda bi, hp, i: (bi * nq + i, hp)),
        scratch_shapes=[pltpu.VMEM((n_chain, 2 * tq, LANES), BF16), stats, stats],
        compiler_params=_params(3),
        name="sb",
    )(qb, kbb, vbb, g_out)


def _block_diag_queries(q, n_heads):
    t, w = q.shape
    head_of_lane = lax.broadcasted_iota(jnp.int32, (n_heads, w), 1) // HEAD_DIM
    hmask = head_of_lane == lax.broadcasted_iota(jnp.int32, (n_heads, w), 0)
    rows = [jnp.where(hmask, jnp.broadcast_to(q[ti:ti + 1, :], (n_heads, w)), 0.0) for ti in range(t)]
    return jnp.concatenate(rows, axis=0), jnp.concatenate([hmask] * t, axis=0)


def _collapse_heads(o, hmask_rows, gain, n_tok, n_heads):
    oh = jnp.where(hmask_rows, o, 0.0)
    ss = jnp.sum(oh * oh, axis=1, keepdims=True) * (1.0 / HEAD_DIM)
    of = oh * lax.rsqrt(ss + LN_EPS)
    rows = [jnp.sum(of[ti * n_heads:(ti + 1) * n_heads, :], axis=0, keepdims=True) for ti in range(n_tok)]
    return jnp.concatenate(rows, axis=0) * gain


def _smoba_kernel(pt_ref, q_ref, kn_ref, vn_ref, go_ref, *rest, n_pages, n_tok, n_heads):
    del pt_ref
    k_pages, v_pages, o_ref = rest[:n_pages], rest[n_pages:2 * n_pages], rest[2 * n_pages]
    qbd, hmask_rows = _block_diag_queries(q_ref[0].astype(F32), n_heads)
    qbd_bf = qbd.astype(BF16)
    kn, vn = kn_ref[0], vn_ref[0]
    tok_of_row = lax.broadcasted_iota(jnp.int32, (n_tok * n_heads, 1), 0) // n_heads
    pages_per_block = MOBA_BLOCK // PAGE_SIZE
    n_blocks = n_pages // pages_per_block
    s_pages = [_dot(qbd_bf, k_pages[p][...].astype(BF16)) for p in range(n_pages)]
    gates = [sum(jnp.sum(s_pages[n * pages_per_block + u], axis=1, keepdims=True) for u in range(pages_per_block))
             for n in range(n_blocks)]
    biases = []
    for n in range(n_blocks):
        rank = jnp.zeros_like(gates[n])
        for n2 in range(n_blocks):
            if n2 != n:
                beats = (gates[n2] > gates[n]) | ((gates[n2] == gates[n]) & (n2 < n))
                rank = rank + beats.astype(F32)
        biases.append(jnp.where(rank < min(MOBA_TOPK, n_blocks), 0.0, NEG))
    s_new = [jnp.where(t <= tok_of_row, jnp.sum(qbd * kn[t:t + 1, :], axis=1, keepdims=True), NEG)
             for t in range(n_tok)]
    s_sel = [s_pages[p] + biases[p // pages_per_block] for p in range(n_pages)]
    m = functools.reduce(jnp.maximum, s_new)
    for p in range(n_pages):
        m = jnp.maximum(m, jnp.max(s_sel[p], axis=1, keepdims=True))
    acc = jnp.zeros(qbd.shape, F32)
    l = jnp.zeros_like(m)
    for t in range(n_tok):
        pt = jnp.exp(s_new[t] - m)
        l = l + pt
        acc = acc + pt * vn[t:t + 1, :]
    for p in range(n_pages):
        pp = jnp.exp(s_sel[p] - m)
        l = l + jnp.sum(pp, axis=1, keepdims=True)
        acc = acc + _dot_nt(pp.astype(BF16), v_pages[p][...].astype(BF16))
    o_ref[0] = _collapse_heads(acc / l, hmask_rows, go_ref[...], n_tok, n_heads).astype(o_ref.dtype)


def _sb_new_tokens(qbd, kn, vn, n_tok, n_heads):
    tok_of_row = lax.broadcasted_iota(jnp.int32, (n_tok * n_heads, 1), 0) // n_heads
    acc = jnp.zeros(qbd.shape, F32)
    carry = jnp.zeros((n_tok * n_heads, 1), F32)
    for t in reversed(range(n_tok)):
        z = jnp.sum(qbd * kn[t:t + 1, :], axis=1, keepdims=True)
        sp = _softplus(z)
        vis = t < tok_of_row
        acc = acc + jnp.where(vis, jnp.exp(z - sp + carry), 0.0) * vn[t:t + 1, :]
        carry = carry - jnp.where(vis, sp, 0.0)
    return acc, carry


def _ssb_recent_kernel(pt_ref, q_ref, kn_ref, vn_ref, go_ref, *rest, n_recent, group, n_tok, n_heads):
    del pt_ref
    n_in = group * n_recent
    k_pages, v_pages = rest[:n_in], rest[n_in:2 * n_in]
    o_ref, car_ref = rest[2 * n_in:]
    later = _later_matrix(n_recent * PAGE_SIZE)
    gather = lambda pages, g: jnp.concatenate([pages[g * n_recent + u][...] for u in range(n_recent)],
                                              axis=1).astype(BF16)
    state = []
    for g in range(group):
        qbd, hmask_rows = _block_diag_queries(q_ref[g].astype(F32), n_heads)
        acc, carry = _sb_new_tokens(qbd, kn_ref[g], vn_ref[g], n_tok, n_heads)
        state.append((hmask_rows, acc, carry, _dot(qbd.astype(BF16), gather(k_pages, g))))
    mids = []
    for hmask_rows, acc, carry, z in state:
        sp = _softplus(z)
        mids.append((z - sp, _dot(sp.astype(BF16), later), jnp.sum(sp, axis=1, keepdims=True)))
    for g in range(group):
        hmask_rows, acc, carry, _ = state[g]
        log_beta, after, drop = mids[g]
        w = jnp.exp(log_beta - after)
        acc = acc + jnp.exp(carry) * _dot_nt(w.astype(BF16), gather(v_pages, g))
        o_ref[g] = _collapse_heads(acc, hmask_rows, go_ref[...], n_tok, n_heads).astype(o_ref.dtype)
        car_ref[g] = jnp.broadcast_to(jnp.max(carry - drop, axis=0, keepdims=True), car_ref.shape[1:])


def _ssb_recent_call(page_table, q, k_new, v_new, g_out, cache_k, cache_v, n_recent, group):
    db, n_pages = page_table.shape
    _, t, w = q.shape
    tok = lambda: pl.BlockSpec((group, t, w), lambda bi, pt: (bi, 0, 0))

    def page(g, u):
        return pl.BlockSpec((None, w, PAGE_SIZE),
                            lambda bi, pt: (pt[bi * group + g, n_pages - n_recent + u], 0, 0))

    pages = [page(g, u) for g in range(group) for u in range(n_recent)]
    grid_spec = pltpu.PrefetchScalarGridSpec(
        num_scalar_prefetch=1,
        grid=(db // group,),
        in_specs=[tok(), tok(), tok(), pl.BlockSpec((1, w), lambda bi, pt: (0, 0))] + pages * 2,
        out_specs=[tok(), pl.BlockSpec((group, 1, LANES), lambda bi, pt: (bi, 0, 0))],
    )
    return pl.pallas_call(
        functools.partial(_ssb_recent_kernel, n_recent=n_recent, group=group, n_tok=t, n_heads=w // HEAD_DIM),
        out_shape=[jax.ShapeDtypeStruct(q.shape, BF16), jax.ShapeDtypeStruct((db, 1, LANES), F32)],
        grid_spec=grid_spec,
        compiler_params=_params(1),
        name="ssb_recent",
    )(page_table, q, k_new, v_new, g_out, *([cache_k] * len(pages)), *([cache_v] * len(pages)))


def _ssb_kernel(pt_ref, q_ref, kn_ref, vn_ref, go_ref, *rest, n_pages, n_tok, n_heads):
    del pt_ref
    k_pages, v_pages = rest[:n_pages], rest[n_pages:2 * n_pages]
    o_ref, acc_ref, car_ref = rest[2 * n_pages:]
    qbd, hmask_rows = _block_diag_queries(q_ref[0].astype(F32), n_heads)
    qbd_bf = qbd.astype(BF16)
    later = _later_matrix(PAGE_SIZE)
    acc, carry = _sb_new_tokens(qbd, kn_ref[0], vn_ref[0], n_tok, n_heads)
    acc_ref[...] = acc
    car_ref[...] = jnp.broadcast_to(carry, car_ref.shape)
    for p in reversed(range(n_pages)):
        @pl.when(jnp.max(car_ref[...]) > UNDERFLOW_LOG)
        def _():
            z = _dot(qbd_bf, k_pages[p][...].astype(BF16))
            sp = _softplus(z)
            w = jnp.exp(z - sp - _dot(sp.astype(BF16), later))
            car = car_ref[...]
            acc_ref[...] += jnp.exp(car[:, :1]) * _dot_nt(w.astype(BF16), v_pages[p][...].astype(BF16))
            car_ref[...] = car - jnp.sum(sp, axis=1, keepdims=True)
    o_ref[0] = _collapse_heads(acc_ref[...], hmask_rows, go_ref[...], n_tok, n_heads).astype(o_ref.dtype)


def _sample_attn_call(body, name, page_table, q, k_new, v_new, g_out, cache_k, cache_v, scratch_shapes=()):
    db, n_pages = page_table.shape
    _, t, w = q.shape
    n_heads = w // HEAD_DIM
    tok = lambda: pl.BlockSpec((1, t, w), lambda bi, pt: (bi, 0, 0))
    page = lambda p: pl.BlockSpec((None, w, PAGE_SIZE), lambda bi, pt: (pt[bi, p], 0, 0))
    grid_spec = pltpu.PrefetchScalarGridSpec(
        num_scalar_prefetch=1,
        grid=(db,),
        in_specs=[tok(), tok(), tok(), pl.BlockSpec((1, w), lambda bi, pt: (0, 0))]
                 + [page(p) for p in range(n_pages)] * 2,
        out_specs=tok(),
        scratch_shapes=scratch_shapes,
    )
    return pl.pallas_call(
        functools.partial(body, n_pages=n_pages, n_tok=t, n_heads=n_heads),
        out_shape=jax.ShapeDtypeStruct(q.shape, BF16),
        grid_spec=grid_spec,
        compiler_params=_params(1),
        name=name,
    )(page_table, q, k_new, v_new, g_out, *([cache_k] * n_pages), *([cache_v] * n_pages))


def _post_kernel(x_ref, g0_ref, b0_ref, ma_ref, mb_ref, mod_ref, wo_ref, g1_ref, b1_ref,
                 wg_ref, wu_ref, wd_ref, g2_ref, b2_ref, o_ref, act_ref, *, alpha, wa, ff_chunk):
    xn = _layer_norm(x_ref[...], g0_ref[...], b0_ref[...])
    attn = _dot(ma_ref[...], wo_ref[:wa, :]) + _dot(mb_ref[...], wo_ref[wa:, :])
    x1 = _layer_norm(alpha * xn + mod_ref[2] * attn, g1_ref[...], b1_ref[...])
    hb = (x1 * (1.0 + mod_ref[4]) + mod_ref[3]).astype(BF16)
    for c in range(0, wg_ref.shape[1], ff_chunk):
        gate = _dot(hb, wg_ref[:, c:c + ff_chunk])
        up = _dot(hb, wu_ref[:, c:c + ff_chunk])
        act_ref[:, c:c + ff_chunk] = (gate / (1.0 + jnp.exp(-gate)) * up).astype(BF16)
    f = _dot(act_ref[...], wd_ref[...])
    o_ref[...] = _layer_norm(alpha * x1 + mod_ref[5] * f, g2_ref[...], b2_ref[...])


def _post_call(x2, g0, b0, ma, mb, mods, mod_spec, wo, g1, b1, wg, wu, wd, g2, b2, alpha, tm, ff_chunk):
    r, d = x2.shape
    wa = ma.shape[1]
    row = lambda w: pl.BlockSpec((tm, w), lambda i: (i, 0))
    vec = lambda: _const_spec((1, d))
    return pl.pallas_call(
        functools.partial(_post_kernel, alpha=alpha, wa=wa, ff_chunk=ff_chunk),
        out_shape=jax.ShapeDtypeStruct((r, d), F32),
        grid=(r // tm,),
        in_specs=[row(d), vec(), vec(), row(wa), row(mb.shape[1]), mod_spec, _const_spec(wo.shape), vec(), vec(),
                  _const_spec(wg.shape), _const_spec(wu.shape), _const_spec(wd.shape), vec(), vec()],
        out_specs=row(d),
        scratch_shapes=[pltpu.VMEM((tm, wg.shape[1]), BF16)],
        compiler_params=_params(1),
        name="post",
    )(x2, g0, b0, ma, mb, mods, wo, g1, b1, wg, wu, wd, g2, b2)


def _rotary_tables(pos):
    half = ROT_DIM // 2
    inv = ROPE_THETA ** (-jnp.arange(half, dtype=F32) * 2.0 / ROT_DIM)
    ang = pos.astype(F32)[:, None] * inv
    cos, sin = jnp.cos(ang), jnp.sin(ang)
    n = pos.shape[0]
    rest = HEAD_DIM - ROT_DIM
    c = jnp.concatenate([cos, cos, jnp.ones((n, rest), F32)], axis=1)
    s_up = jnp.concatenate([-sin, jnp.zeros((n, half + rest), F32)], axis=1)
    s_dn = jnp.concatenate([jnp.zeros((n, half), F32), sin, jnp.zeros((n, rest), F32)], axis=1)
    return jnp.stack([jnp.tile(t, (1, HEADS_PER_TILE)) for t in (c, s_up, s_dn)])


def kernel(x_prompt, x_sample, cache_moba_k, cache_moba_v, cache_sb_k, cache_sb_v, page_table, c_prompt, c_sample,
           ln0_g, ln0_b, w_mod, b_mod, w_in, g_out_a, g_out_b, w_out, ln1_g, ln1_b, w_gate, w_up, w_down,
           ln2_g, ln2_b):
    b, s, d = x_prompt.shape
    db, t, _ = x_sample.shape
    depth = w_mod.shape[0]
    assert depth == 1, "ln0 is fused into the first layer's kernels; only a one-layer trunk is supported"
    wa, wb = g_out_a.shape[1], g_out_b.shape[1]
    ha, hb = wa // HEAD_DIM, wb // HEAD_DIM
    n_pages = page_table.shape[1]
    past = n_pages * PAGE_SIZE
    assert s % MOBA_BLOCK == 0 and past % MOBA_BLOCK == 0
    alpha = (2 * depth) ** 0.25
    vec = lambda a: a.reshape(1, -1)

    n_mod = 6
    rows = b + db
    pad = -rows % 8
    c_all = jnp.concatenate([c_prompt, c_sample, jnp.zeros((pad, d), F32)], axis=0)
    m = _mod_call(c_all, w_mod[0], b_mod[0].reshape(1, -1))
    mods_p = m[:b].reshape(b, n_mod, 1, d)
    mods_s = jnp.repeat(m[b:rows].reshape(db, n_mod, d), t, axis=0).transpose(1, 0, 2)

    w_in_bf = w_in[0].astype(BF16)
    wo_bf, wg_bf, wu_bf, wd_bf = (w[0].astype(BF16) for w in (w_out, w_gate, w_up, w_down))
    g0, b0 = vec(ln0_g), vec(ln0_b)
    ln = [vec(a[0]) for a in (ln1_g, ln1_b, ln2_g, ln2_b)]
    goa, gob = g_out_a[0].reshape(1, wa), g_out_b[0].reshape(1, wb)

    tm = 512
    ff_chunk = 256

    xp2 = x_prompt.reshape(b * s, d)
    mod_spec_p = pl.BlockSpec((None, n_mod, 1, d), lambda i: (i // (s // tm), 0, 0, 0))
    rot_p = _rotary_tables(jnp.arange(s, dtype=jnp.int32))
    (qa, qb, ka, va, kb, vb, kab, vab, kbb, vbb, kmean) = _pre_call(
        xp2, g0, b0, mods_p, mod_spec_p, rot_p, w_in_bf, wa, wb, tm, True)
    mixed_a = _moba_call(qa, kab, vab, kmean.reshape(b, s // MOBA_BLOCK, wa), goa, b, s, n_chain=4)
    mixed_b = _sb_call(qb, kbb, vbb, gob, b, s, n_chain=4)
    y_p = _post_call(xp2, g0, b0, mixed_a, mixed_b, mods_p, mod_spec_p, wo_bf, ln[0], ln[1],
                     wg_bf, wu_bf, wd_bf, ln[2], ln[3], alpha, tm, ff_chunk)

    rs = db * t
    xs2 = x_sample.reshape(rs, d)
    mod_spec_s = pl.BlockSpec((n_mod, rs, d), lambda i: (0, 0, 0))
    rot_s = _rotary_tables(jnp.tile(past + jnp.arange(t, dtype=jnp.int32), db))
    (sqa, sqb, ska, sva, skb, svb) = _pre_call(
        xs2, g0, b0, mods_s, mod_spec_s, rot_s, w_in_bf, wa, wb, rs, False)
    tok3 = lambda a: a.reshape(db, t, a.shape[-1])
    pages = lambda c: c[0].transpose(0, 2, 3, 1).reshape(c.shape[1], -1, PAGE_SIZE)
    smixed_a = _sample_attn_call(_smoba_kernel, "smoba", page_table, tok3(sqa), tok3(ska), tok3(sva), goa,
                                 pages(cache_moba_k), pages(cache_moba_v))
    sb_args = (page_table, tok3(sqb), tok3(skb), tok3(svb), gob, pages(cache_sb_k), pages(cache_sb_v))
    recent_b, log_keep = _ssb_recent_call(*sb_args, n_recent=min(SB_RECENT_PAGES, n_pages), group=SB_GROUP)
    smixed_b = lax.cond(
        jnp.max(log_keep) > UNDERFLOW_LOG,
        lambda: _sample_attn_call(_ssb_kernel, "ssb", *sb_args, scratch_shapes=[
            pltpu.VMEM((t * hb, wb), F32), pltpu.VMEM((t * hb, LANES), F32)]),
        lambda: recent_b)
    y_s = _post_call(xs2, g0, b0, smixed_a.reshape(rs, wa), smixed_b.reshape(rs, wb), mods_s, mod_spec_s,
                     wo_bf, ln[0], ln[1], wg_bf, wu_bf, wd_bf, ln[2], ln[3], alpha, rs, ff_chunk)

    kv_p = lambda a, h: a.reshape(b, h, HEAD_DIM, s).transpose(0, 3, 1, 2)[None]
    return (y_p.reshape(b, s, d), y_s.reshape(db, t, d),
            kv_p(ka, ha), kv_p(va, ha), kv_p(kb, hb), kv_p(vb, hb),
            ska.reshape(1, db, t, ha, HEAD_DIM), sva.reshape(1, db, t, ha, HEAD_DIM),
            skb.reshape(1, db, t, hb, HEAD_DIM), svb.reshape(1, db, t, hb, HEAD_DIM))
```

```python
import functools

import jax
import jax.numpy as jnp
from jax import lax
from jax.experimental import pallas as pl
from jax.experimental.pallas import tpu as pltpu

F32 = jnp.float32
BF16 = jnp.bfloat16

HEAD_DIM = 64
MOBA_BLOCK = 256
MOBA_TOPK = 3
PAGE_SIZE = 128
ROPE_THETA = 500000.0
ROT_DIM = HEAD_DIM // 4
LN_EPS = 1e-5
LANES = 128
HEADS_PER_TILE = LANES // HEAD_DIM
NEG = -1e30
UNDERFLOW_LOG = -110.0
MOBA_SCORE_LEAD = 4
SB_RECENT_PAGES = 2
SB_GROUP = 4
VMEM_LIMIT = 56 * 1024 * 1024


def _dot(a, b):
    return jnp.dot(a, b, preferred_element_type=F32)


def _dot_nt(a, b):
    return lax.dot_general(a, b, (((1,), (1,)), ((), ())), preferred_element_type=F32)


def _dot_tn(a, b):
    return lax.dot_general(a, b, (((0,), (0,)), ((), ())), preferred_element_type=F32)


def _split_bf16(x):
    hi = x.astype(BF16)
    lo = (x - hi.astype(F32)).astype(BF16)
    return hi, lo


def _layer_norm(x, g, b):
    mu = jnp.mean(x, axis=-1, keepdims=True)
    xc = x - mu
    var = jnp.mean(xc * xc, axis=-1, keepdims=True)
    return xc * lax.rsqrt(var + LN_EPS) * g + b


def _softplus(z):
    return jnp.maximum(z, 0.0) + jnp.log(1.0 + jnp.exp(-jnp.abs(z)))


def _later_matrix(n):
    return (lax.broadcasted_iota(jnp.int32, (n, n), 0) > lax.broadcasted_iota(jnp.int32, (n, n), 1)).astype(BF16)


def _params(n_grid):
    return pltpu.CompilerParams(dimension_semantics=("arbitrary",) * n_grid, vmem_limit_bytes=VMEM_LIMIT)


def _const_spec(shape):
    zeros = (0,) * len(shape)
    return pl.BlockSpec(shape, lambda *_: zeros, pipeline_mode=pl.Buffered(1))


def _mod_kernel(c_ref, w_ref, b_ref, o_ref):
    c = c_ref[...]
    a = c / (1.0 + jnp.exp(-c))
    o_ref[...] = _dot(a.astype(BF16), w_ref[...].astype(BF16)) + b_ref[...]


def _mod_call(c_all, w_mod, b_mod, tn=1536):
    r, d = c_all.shape
    n = w_mod.shape[1]
    return pl.pallas_call(
        _mod_kernel,
        out_shape=jax.ShapeDtypeStruct((r, n), F32),
        grid=(n // tn,),
        in_specs=[pl.BlockSpec((r, d), lambda j: (0, 0)),
                  pl.BlockSpec((d, tn), lambda j: (0, j)),
                  pl.BlockSpec((1, tn), lambda j: (0, j))],
        out_specs=pl.BlockSpec((r, tn), lambda j: (0, j)),
        compiler_params=_params(1),
        name="mod",
    )(c_all, w_mod, b_mod)


def _pre_kernel(x_ref, g0_ref, b0_ref, mod_ref, rot_ref, w_ref, qa_ref, qb_ref, ka_ref, va_ref, kb_ref, vb_ref,
                *extra_refs, wa, wb):
    transposed = bool(extra_refs)

    def put(ref, sl, val):
        if transposed:
            ref[sl, :] = val.T
        else:
            ref[:, sl] = val

    xn = _layer_norm(x_ref[...], g0_ref[...], b0_ref[...])
    h = xn * (1.0 + mod_ref[1]) + mod_ref[0]
    qkv = _dot(h.astype(BF16), w_ref[...])
    cos, s_up, s_dn = rot_ref[0], rot_ref[1], rot_ref[2]

    def rotate(t):
        return t * cos + pltpu.roll(t, LANES - ROT_DIM // 2, 1) * s_up + pltpu.roll(t, ROT_DIM // 2, 1) * s_dn

    for t in range(wa // LANES):
        sl = slice(t * LANES, (t + 1) * LANES)
        qa = rotate(qkv[:, sl])
        ka = rotate(qkv[:, wa + t * LANES:wa + (t + 1) * LANES])
        qa_ref[:, sl] = (qa * HEAD_DIM ** -0.5).astype(BF16)
        put(ka_ref, sl, ka)
        if extra_refs:
            kab_ref, km_ref = extra_refs[0], extra_refs[4]
            kab_ref[:, sl] = ka.astype(BF16)
            for g in range(km_ref.shape[0]):
                blk = ka[g * MOBA_BLOCK:(g + 1) * MOBA_BLOCK, :]
                km_ref[g, :, sl] = jnp.sum(blk, axis=0, keepdims=True) * (1.0 / MOBA_BLOCK)
    o = 3 * wa
    va, kb, vb = qkv[:, 2 * wa:o], qkv[:, o + wb:o + 2 * wb], qkv[:, o + 2 * wb:o + 3 * wb]
    qb_ref[...] = (qkv[:, o:o + wb] * HEAD_DIM ** -0.5).astype(BF16)
    for ref, val in ((va_ref, va), (kb_ref, kb), (vb_ref, vb)):
        for t in range(val.shape[1] // LANES):
            sl = slice(t * LANES, (t + 1) * LANES)
            put(ref, sl, val[:, sl])
    if extra_refs:
        extra_refs[1][...] = va.astype(BF16)
        extra_refs[2][...] = kb.astype(BF16)
        extra_refs[3][...] = vb.astype(BF16)


def _pre_call(x2, g0, b0, mods, mod_spec, rot, w_in_bf, wa, wb, tm, prompt):
    r, d = x2.shape
    n_rot = rot.shape[1] // tm
    row = lambda w: pl.BlockSpec((tm, w), lambda i: (i, 0))
    if prompt:
        nt = rot.shape[1] // tm
        kv = lambda w: jax.ShapeDtypeStruct((r // rot.shape[1], w, rot.shape[1]), F32)
        kv_spec = lambda w: pl.BlockSpec((None, w, tm), lambda i: (i // nt, 0, i % nt))
    else:
        kv = lambda w: jax.ShapeDtypeStruct((r, w), F32)
        kv_spec = row
    outs = [jax.ShapeDtypeStruct((r, wa), BF16), jax.ShapeDtypeStruct((r, wb), BF16), kv(wa), kv(wa), kv(wb), kv(wb)]
    out_specs = [row(wa), row(wb), kv_spec(wa), kv_spec(wa), kv_spec(wb), kv_spec(wb)]
    if prompt:
        outs += [jax.ShapeDtypeStruct((r, wa), BF16), jax.ShapeDtypeStruct((r, wa), BF16),
                 jax.ShapeDtypeStruct((r, wb), BF16), jax.ShapeDtypeStruct((r, wb), BF16),
                 jax.ShapeDtypeStruct((r // MOBA_BLOCK, 1, wa), F32)]
        out_specs += [row(wa), row(wa), row(wb), row(wb),
                      pl.BlockSpec((tm // MOBA_BLOCK, 1, wa), lambda i: (i, 0, 0))]
    return pl.pallas_call(
        functools.partial(_pre_kernel, wa=wa, wb=wb),
        out_shape=outs,
        grid=(r // tm,),
        in_specs=[row(d), _const_spec((1, d)), _const_spec((1, d)), mod_spec,
                  pl.BlockSpec((3, tm, LANES), lambda i: (0, i % n_rot, 0)),
                  _const_spec(w_in_bf.shape)],
        out_specs=out_specs,
        compiler_params=_params(1),
        name="pre",
    )(x2, g0, b0, mods, rot, w_in_bf)


def _merge_heads_and_norm(o2, gain, tq):
    lane = lax.broadcasted_iota(jnp.int32, (1, LANES), 1)
    first = lane < HEAD_DIM
    o = jnp.where(first, o2[:tq], o2[tq:])
    sq = o * o
    ss0 = jnp.sum(jnp.where(first, sq, 0.0), axis=1, keepdims=True)
    ss1 = jnp.sum(jnp.where(first, 0.0, sq), axis=1, keepdims=True)
    scale = jnp.where(first, lax.rsqrt(ss0 * (1.0 / HEAD_DIM) + LN_EPS), lax.rsqrt(ss1 * (1.0 / HEAD_DIM) + LN_EPS))
    return o * scale * gain


def _moba_kernel(q_ref, k_ref, v_ref, km_ref, go_ref, o_ref, q2_ref, s_ref, acc_ref, m_ref, *,
                 tq, nblk, n_chain):
    i = pl.program_id(2)
    i_f = i.astype(F32)
    lane = lax.broadcasted_iota(jnp.int32, (1, LANES), 1)
    n_idx = lax.broadcasted_iota(jnp.int32, (nblk, tq), 0).astype(F32)
    valid = n_idx < i_f
    place = (lax.broadcasted_iota(jnp.int32, (nblk, LANES), 1)
             == lax.broadcasted_iota(jnp.int32, (nblk, LANES), 0)).astype(BF16)
    gates = []
    for c in range(n_chain):
        tile = slice(c * LANES, (c + 1) * LANES)
        q = q_ref[:, tile]
        for hh in range(HEADS_PER_TILE):
            q2_ref[c, hh * tq:(hh + 1) * tq, :LANES] = jnp.where((lane // HEAD_DIM) == hh, q, jnp.zeros_like(q))
        g2 = _dot_nt(jnp.concatenate(_split_bf16(km_ref[0, :, tile]), axis=0), q2_ref[c, :, :LANES])
        gates.append(g2[:nblk] + g2[nblk:])
    biases = []
    for c in range(n_chain):
        halves = []
        for hh in range(HEADS_PER_TILE):
            g = gates[c][:, hh * tq:(hh + 1) * tq]
            cand = valid
            sel = n_idx == i_f
            for _ in range(MOBA_TOPK):
                gm = jnp.where(cand, g, -jnp.inf)
                best = jnp.max(gm, axis=0, keepdims=True)
                first = jnp.min(jnp.where(cand & (gm == best), n_idx, float(nblk)), axis=0, keepdims=True)
                pick = n_idx == first
                sel = sel | pick
                cand = cand & jnp.logical_not(pick)
            halves.append(jnp.where(sel, 0.0, NEG).astype(BF16))
        biases.append(jnp.concatenate(halves, axis=1))
    for c in range(n_chain):
        q2_ref[c, :, LANES:] = _dot_tn(biases[c], place).astype(BF16)
    row_in_tile = lax.broadcasted_iota(jnp.int32, (2 * tq, 1), 0) & (tq - 1)
    kofs = lax.broadcasted_iota(jnp.int32, (1, MOBA_BLOCK), 1)

    first_head = lane < HEAD_DIM

    acc_ref[...] = jnp.zeros_like(acc_ref)
    m_ref[...] = jnp.full_like(m_ref, NEG)

    def score(c, j):
        ks = k_ref[pl.ds(pl.multiple_of(j * MOBA_BLOCK, MOBA_BLOCK), MOBA_BLOCK), c * LANES:(c + 1) * LANES]
        onehot = jnp.broadcast_to((lane == j).astype(BF16), (MOBA_BLOCK, LANES))
        return _dot_nt(q2_ref[c], jnp.concatenate([ks, onehot], axis=1))

    lead = s_ref.shape[0]
    for c in range(lead):
        s_ref[c] = score(c, 0)

    def step(j, causal):
        start = pl.multiple_of(j * MOBA_BLOCK, MOBA_BLOCK)
        scores = [s_ref[c] for c in range(lead)]
        for c in range(n_chain):
            if c + lead < n_chain:
                scores.append(score(c + lead, j))
            elif not causal:
                s_ref[c + lead - n_chain] = score(c + lead - n_chain, j + 1)
            vs = v_ref[pl.ds(start, MOBA_BLOCK), c * LANES:(c + 1) * LANES]
            s = scores[c]
            if causal:
                s = jnp.where(kofs <= row_in_tile, s, NEG)
            m_prev = m_ref[c]
            m_new = jnp.maximum(m_prev, jnp.max(s, axis=1, keepdims=True))
            alpha = jnp.exp(m_prev - m_new)
            p = jnp.exp(s - jnp.concatenate([m_new] * (MOBA_BLOCK // LANES), axis=1)).astype(BF16)
            ones = jnp.ones_like(vs)
            pv = [_dot(p[:tq], jnp.where(first_head, vs, ones)), _dot(p[tq:], jnp.where(first_head, ones, vs))]
            acc_ref[c] = alpha * acc_ref[c] + jnp.concatenate(pv, axis=0)
            m_ref[c] = m_new

    def past_step(j, carry):
        step(j, False)
        return carry

    lax.fori_loop(0, i, past_step, 0)
    step(i, True)
    for c in range(n_chain):
        tile = slice(c * LANES, (c + 1) * LANES)
        acc = acc_ref[c]
        den = jnp.concatenate([jnp.max(jnp.where(first_head, 0.0, acc[:tq]), axis=1, keepdims=True),
                               jnp.max(jnp.where(first_head, acc[tq:], 0.0), axis=1, keepdims=True)], axis=0)
        o_ref[:, tile] = _merge_heads_and_norm(acc / den, go_ref[:, tile], tq).astype(o_ref.dtype)


def _moba_call(qa, kab, vab, kmean, g_out, b, s, n_chain):
    tq = MOBA_BLOCK
    nq = s // tq
    nblk = s // MOBA_BLOCK
    w = n_chain * LANES
    n_groups = qa.shape[1] // w
    stats = pltpu.VMEM((n_chain, 2 * tq, LANES), F32)
    return pl.pallas_call(
        functools.partial(_moba_kernel, tq=tq, nblk=nblk, n_chain=n_chain),
        out_shape=jax.ShapeDtypeStruct(qa.shape, BF16),
        grid=(b, n_groups, nq),
        in_specs=[pl.BlockSpec((tq, w), lambda bi, hp, i: (bi * nq + i, hp)),
                  pl.BlockSpec((s, w), lambda bi, hp, i: (bi, hp)),
                  pl.BlockSpec((s, w), lambda bi, hp, i: (bi, hp)),
                  pl.BlockSpec((1, nblk, w), lambda bi, hp, i: (bi, 0, hp)),
                  pl.BlockSpec((1, w), lambda bi, hp, i: (0, hp))],
        out_specs=pl.BlockSpec((tq, w), lambda bi, hp, i: (bi * nq + i, hp)),
        scratch_shapes=[pltpu.VMEM((n_chain, 2 * tq, 2 * LANES), BF16),
                        pltpu.VMEM((MOBA_SCORE_LEAD, 2 * tq, MOBA_BLOCK), F32), stats, stats],
        compiler_params=_params(3),
        name="moba",
    )(qa, kab, vab, kmean, g_out)


def _sb_kernel(q_ref, k_ref, v_ref, go_ref, o_ref, q2_ref, acc_ref, car_ref, *, tq, n_chain):
    i = pl.program_id(2)
    lane = lax.broadcasted_iota(jnp.int32, (1, LANES), 1)
    for c in range(n_chain):
        q = q_ref[:, c * LANES:(c + 1) * LANES]
        for hh in range(HEADS_PER_TILE):
            q2_ref[c, hh * tq:(hh + 1) * tq, :] = jnp.where((lane // HEAD_DIM) == hh, q, jnp.zeros_like(q))
    row_in_tile = lax.broadcasted_iota(jnp.int32, (2 * tq, 1), 0) & (tq - 1)
    kofs = lax.broadcasted_iota(jnp.int32, (1, MOBA_BLOCK), 1)
    later = _later_matrix(MOBA_BLOCK)

    def step(j, first):
        start = pl.multiple_of(j * MOBA_BLOCK, MOBA_BLOCK)
        vis = kofs < row_in_tile
        zs = [_dot_nt(q2_ref[c], k_ref[pl.ds(start, MOBA_BLOCK), c * LANES:(c + 1) * LANES]) for c in range(n_chain)]
        mids = []
        for c in range(n_chain):
            sp = _softplus(zs[c])
            spk = jnp.where(vis, sp, 0.0) if first else sp
            mids.append((zs[c] - sp, _dot(spk.astype(BF16), later), jnp.sum(spk, axis=1, keepdims=True)))
        for c in range(n_chain):
            log_beta, after, drop = mids[c]
            w = jnp.exp(log_beta - after)
            if first:
                w = jnp.where(vis, w, 0.0)
            pv = _dot(w.astype(BF16), v_ref[pl.ds(start, MOBA_BLOCK), c * LANES:(c + 1) * LANES])
            if first:
                acc_ref[c] = pv
                car_ref[c] = jnp.broadcast_to(-drop, pv.shape)
            else:
                car = car_ref[c]
                acc_ref[c] += jnp.exp(car) * pv
                car_ref[c] = car - drop

    step(i, True)

    def cond(state):
        t, car_max = state
        return jnp.logical_and(t < i, car_max > UNDERFLOW_LOG)

    def body(state):
        t, _ = state
        step(i - 1 - t, False)
        return t + 1, jnp.max(car_ref[...])

    lax.while_loop(cond, body, (jnp.int32(0), jnp.max(car_ref[...])))
    for c in range(n_chain):
        tile = slice(c * LANES, (c + 1) * LANES)
        o_ref[:, tile] = _merge_heads_and_norm(acc_ref[c], go_ref[:, tile], tq).astype(o_ref.dtype)


def _sb_call(qb, kbb, vbb, g_out, b, s, n_chain):
    tq = MOBA_BLOCK
    nq = s // tq
    w = n_chain * LANES
    n_groups = qb.shape[1] // w
    stats = pltpu.VMEM((n_chain, 2 * tq, LANES), F32)
    return pl.pallas_call(
        functools.partial(_sb_kernel, tq=tq, n_chain=n_chain),
        out_shape=jax.ShapeDtypeStruct(qb.shape, BF16),
        grid=(b, n_groups, nq),
        in_specs=[pl.BlockSpec((tq, w), lambda bi, hp, i: (bi * nq + i, hp)),
                  pl.BlockSpec((s, w), lambda bi, hp, i: (bi, hp)),
                  pl.BlockSpec((s, w), lambda bi, hp, i: (bi, hp)),
                  pl.BlockSpec((1, w), lambda bi, hp, i: (0, hp))],
        out_specs=pl.BlockSpec((tq, w), lambda bi, hp, i: (bi * nq + i, hp)),
        scratch_shapes=[pltpu.VMEM((n_chain, 2 * tq, LANES), BF16), stats, stats],
        compiler_params=_params(3),
        name="sb",
    )(qb, kbb, vbb, g_out)


def _block_diag_queries(q, n_heads):
    t, w = q.shape
    head_of_lane = lax.broadcasted_iota(jnp.int32, (n_heads, w), 1) // HEAD_DIM
    hmask = head_of_lane == lax.broadcasted_iota(jnp.int32, (n_heads, w), 0)
    rows = [jnp.where(hmask, jnp.broadcast_to(q[ti:ti + 1, :], (n_heads, w)), 0.0) for ti in range(t)]
    return jnp.concatenate(rows, axis=0), jnp.concatenate([hmask] * t, axis=0)


def _collapse_heads(o, hmask_rows, gain, n_tok, n_heads):
    oh = jnp.where(hmask_rows, o, 0.0)
    ss = jnp.sum(oh * oh, axis=1, keepdims=True) * (1.0 / HEAD_DIM)
    of = oh * lax.rsqrt(ss + LN_EPS)
    rows = [jnp.sum(of[ti * n_heads:(ti + 1) * n_heads, :], axis=0, keepdims=True) for ti in range(n_tok)]
    return jnp.concatenate(rows, axis=0) * gain


def _smoba_kernel(pt_ref, q_ref, kn_ref, vn_ref, go_ref, *rest, n_pages, n_tok, n_heads):
    del pt_ref
    k_pages, v_pages, o_ref = rest[:n_pages], rest[n_pages:2 * n_pages], rest[2 * n_pages]
    qbd, hmask_rows = _block_diag_queries(q_ref[0].astype(F32), n_heads)
    qbd_bf = qbd.astype(BF16)
    kn, vn = kn_ref[0], vn_ref[0]
    tok_of_row = lax.broadcasted_iota(jnp.int32, (n_tok * n_heads, 1), 0) // n_heads
    pages_per_block = MOBA_BLOCK // PAGE_SIZE
    n_blocks = n_pages // pages_per_block
    s_pages = [_dot(qbd_bf, k_pages[p][...].astype(BF16)) for p in range(n_pages)]
    gates = [sum(jnp.sum(s_pages[n * pages_per_block + u], axis=1, keepdims=True) for u in range(pages_per_block))
             for n in range(n_blocks)]
    biases = []
    for n in range(n_blocks):
        rank = jnp.zeros_like(gates[n])
        for n2 in range(n_blocks):
            if n2 != n:
                beats = (gates[n2] > gates[n]) | ((gates[n2] == gates[n]) & (n2 < n))
                rank = rank + beats.astype(F32)
        biases.append(jnp.where(rank < min(MOBA_TOPK, n_blocks), 0.0, NEG))
    s_new = [jnp.where(t <= tok_of_row, jnp.sum(qbd * kn[t:t + 1, :], axis=1, keepdims=True), NEG)
             for t in range(n_tok)]
    s_sel = [s_pages[p] + biases[p // pages_per_block] for p in range(n_pages)]
    m = functools.reduce(jnp.maximum, s_new)
    for p in range(n_pages):
        m = jnp.maximum(m, jnp.max(s_sel[p], axis=1, keepdims=True))
    acc = jnp.zeros(qbd.shape, F32)
    l = jnp.zeros_like(m)
    for t in range(n_tok):
        pt = jnp.exp(s_new[t] - m)
        l = l + pt
        acc = acc + pt * vn[t:t + 1, :]
    for p in range(n_pages):
        pp = jnp.exp(s_sel[p] - m)
        l = l + jnp.sum(pp, axis=1, keepdims=True)
        acc = acc + _dot_nt(pp.astype(BF16), v_pages[p][...].astype(BF16))
    o_ref[0] = _collapse_heads(acc / l, hmask_rows, go_ref[...], n_tok, n_heads).astype(o_ref.dtype)


def _sb_new_tokens(qbd, kn, vn, n_tok, n_heads):
    tok_of_row = lax.broadcasted_iota(jnp.int32, (n_tok * n_heads, 1), 0) // n_heads
    acc = jnp.zeros(qbd.shape, F32)
    carry = jnp.zeros((n_tok * n_heads, 1), F32)
    for t in reversed(range(n_tok)):
        z = jnp.sum(qbd * kn[t:t + 1, :], axis=1, keepdims=True)
        sp = _softplus(z)
        vis = t < tok_of_row
        acc = acc + jnp.where(vis, jnp.exp(z - sp + carry), 0.0) * vn[t:t + 1, :]
        carry = carry - jnp.where(vis, sp, 0.0)
    return acc, carry


def _ssb_recent_kernel(pt_ref, q_ref, kn_ref, vn_ref, go_ref, *rest, n_recent, group, n_tok, n_heads):
    del pt_ref
    n_in = group * n_recent
    k_pages, v_pages = rest[:n_in], rest[n_in:2 * n_in]
    o_ref, car_ref = rest[2 * n_in:]
    later = _later_matrix(n_recent * PAGE_SIZE)
    gather = lambda pages, g: jnp.concatenate([pages[g * n_recent + u][...] for u in range(n_recent)],
                                              axis=1).astype(BF16)
    state = []
    for g in range(group):
        qbd, hmask_rows = _block_diag_queries(q_ref[g].astype(F32), n_heads)
        acc, carry = _sb_new_tokens(qbd, kn_ref[g], vn_ref[g], n_tok, n_heads)
        state.append((hmask_rows, acc, carry, _dot(qbd.astype(BF16), gather(k_pages, g))))
    mids = []
    for hmask_rows, acc, carry, z in state:
        sp = _softplus(z)
        mids.append((z - sp, _dot(sp.astype(BF16), later), jnp.sum(sp, axis=1, keepdims=True)))
    for g in range(group):
        hmask_rows, acc, carry, _ = state[g]
        log_beta, after, drop = mids[g]
        w = jnp.exp(log_beta - after)
        acc = acc + jnp.exp(carry) * _dot_nt(w.astype(BF16), gather(v_pages, g))
        o_ref[g] = _collapse_heads(acc, hmask_rows, go_ref[...], n_tok, n_heads).astype(o_ref.dtype)
        car_ref[g] = jnp.broadcast_to(jnp.max(carry - drop, axis=0, keepdims=True), car_ref.shape[1:])


def _ssb_recent_call(page_table, q, k_new, v_new, g_out, cache_k, cache_v, n_recent, group):
    db, n_pages = page_table.shape
    _, t, w = q.shape
    tok = lambda: pl.BlockSpec((group, t, w), lambda bi, pt: (bi, 0, 0))

    def page(g, u):
        return pl.BlockSpec((None, w, PAGE_SIZE),
                            lambda bi, pt: (pt[bi * group + g, n_pages - n_recent + u], 0, 0))

    pages = [page(g, u) for g in range(group) for u in range(n_recent)]
    grid_spec = pltpu.PrefetchScalarGridSpec(
        num_scalar_prefetch=1,
        grid=(db // group,),
        in_specs=[tok(), tok(), tok(), pl.BlockSpec((1, w), lambda bi, pt: (0, 0))] + pages * 2,
        out_specs=[tok(), pl.BlockSpec((group, 1, LANES), lambda bi, pt: (bi, 0, 0))],
    )
    return pl.pallas_call(
        functools.partial(_ssb_recent_kernel, n_recent=n_recent, group=group, n_tok=t, n_heads=w // HEAD_DIM),
        out_shape=[jax.ShapeDtypeStruct(q.shape, BF16), jax.ShapeDtypeStruct((db, 1, LANES), F32)],
        grid_spec=grid_spec,
        compiler_params=_params(1),
        name="ssb_recent",
    )(page_table, q, k_new, v_new, g_out, *([cache_k] * len(pages)), *([cache_v] * len(pages)))


def _ssb_kernel(pt_ref, q_ref, kn_ref, vn_ref, go_ref, *rest, n_pages, n_tok, n_heads):
    del pt_ref
    k_pages, v_pages = rest[:n_pages], rest[n_pages:2 * n_pages]
    o_ref, acc_ref, car_ref = rest[2 * n_pages:]
    qbd, hmask_rows = _block_diag_queries(q_ref[0].astype(F32), n_heads)
    qbd_bf = qbd.astype(BF16)
    later = _later_matrix(PAGE_SIZE)
    acc, carry = _sb_new_tokens(qbd, kn_ref[0], vn_ref[0], n_tok, n_heads)
    acc_ref[...] = acc
    car_ref[...] = jnp.broadcast_to(carry, car_ref.shape)
    for p in reversed(range(n_pages)):
        @pl.when(jnp.max(car_ref[...]) > UNDERFLOW_LOG)
        def _():
            z = _dot(qbd_bf, k_pages[p][...].astype(BF16))
            sp = _softplus(z)
            w = jnp.exp(z - sp - _dot(sp.astype(BF16), later))
            car = car_ref[...]
            acc_ref[...] += jnp.exp(car[:, :1]) * _dot_nt(w.astype(BF16), v_pages[p][...].astype(BF16))
            car_ref[...] = car - jnp.sum(sp, axis=1, keepdims=True)
    o_ref[0] = _collapse_heads(acc_ref[...], hmask_rows, go_ref[...], n_tok, n_heads).astype(o_ref.dtype)


def _sample_attn_call(body, name, page_table, q, k_new, v_new, g_out, cache_k, cache_v, scratch_shapes=()):
    db, n_pages = page_table.shape
    _, t, w = q.shape
    n_heads = w // HEAD_DIM
    tok = lambda: pl.BlockSpec((1, t, w), lambda bi, pt: (bi, 0, 0))
    page = lambda p: pl.BlockSpec((None, w, PAGE_SIZE), lambda bi, pt: (pt[bi, p], 0, 0))
    grid_spec = pltpu.PrefetchScalarGridSpec(
        num_scalar_prefetch=1,
        grid=(db,),
        in_specs=[tok(), tok(), tok(), pl.BlockSpec((1, w), lambda bi, pt: (0, 0))]
                 + [page(p) for p in range(n_pages)] * 2,
        out_specs=tok(),
        scratch_shapes=scratch_shapes,
    )
    return pl.pallas_call(
        functools.partial(body, n_pages=n_pages, n_tok=t, n_heads=n_heads),
        out_shape=jax.ShapeDtypeStruct(q.shape, BF16),
        grid_spec=grid_spec,
        compiler_params=_params(1),
        name=name,
    )(page_table, q, k_new, v_new, g_out, *([cache_k] * n_pages), *([cache_v] * n_pages))


def _post_kernel(x_ref, g0_ref, b0_ref, ma_ref, mb_ref, mod_ref, wo_ref, g1_ref, b1_ref,
                 wg_ref, wu_ref, wd_ref, g2_ref, b2_ref, o_ref, act_ref, *, alpha, wa, ff_chunk):
    xn = _layer_norm(x_ref[...], g0_ref[...], b0_ref[...])
    attn = _dot(ma_ref[...], wo_ref[:wa, :]) + _dot(mb_ref[...], wo_ref[wa:, :])
    x1 = _layer_norm(alpha * xn + mod_ref[2] * attn, g1_ref[...], b1_ref[...])
    hb = (x1 * (1.0 + mod_ref[4]) + mod_ref[3]).astype(BF16)
    for c in range(0, wg_ref.shape[1], ff_chunk):
        gate = _dot(hb, wg_ref[:, c:c + ff_chunk])
        up = _dot(hb, wu_ref[:, c:c + ff_chunk])
        act_ref[:, c:c + ff_chunk] = (gate / (1.0 + jnp.exp(-gate)) * up).astype(BF16)
    f = _dot(act_ref[...], wd_ref[...])
    o_ref[...] = _layer_norm(alpha * x1 + mod_ref[5] * f, g2_ref[...], b2_ref[...])


def _post_call(x2, g0, b0, ma, mb, mods, mod_spec, wo, g1, b1, wg, wu, wd, g2, b2, alpha, tm, ff_chunk):
    r, d = x2.shape
    wa = ma.shape[1]
    row = lambda w: pl.BlockSpec((tm, w), lambda i: (i, 0))
    vec = lambda: _const_spec((1, d))
    return pl.pallas_call(
        functools.partial(_post_kernel, alpha=alpha, wa=wa, ff_chunk=ff_chunk),
        out_shape=jax.ShapeDtypeStruct((r, d), F32),
        grid=(r // tm,),
        in_specs=[row(d), vec(), vec(), row(wa), row(mb.shape[1]), mod_spec, _const_spec(wo.shape), vec(), vec(),
                  _const_spec(wg.shape), _const_spec(wu.shape), _const_spec(wd.shape), vec(), vec()],
        out_specs=row(d),
        scratch_shapes=[pltpu.VMEM((tm, wg.shape[1]), BF16)],
        compiler_params=_params(1),
        name="post",
    )(x2, g0, b0, ma, mb, mods, wo, g1, b1, wg, wu, wd, g2, b2)


def _rotary_tables(pos):
    half = ROT_DIM // 2
    inv = ROPE_THETA ** (-jnp.arange(half, dtype=F32) * 2.0 / ROT_DIM)
    ang = pos.astype(F32)[:, None] * inv
    cos, sin = jnp.cos(ang), jnp.sin(ang)
    n = pos.shape[0]
    rest = HEAD_DIM - ROT_DIM
    c = jnp.concatenate([cos, cos, jnp.ones((n, rest), F32)], axis=1)
    s_up = jnp.concatenate([-sin, jnp.zeros((n, half + rest), F32)], axis=1)
    s_dn = jnp.concatenate([jnp.zeros((n, half), F32), sin, jnp.zeros((n, rest), F32)], axis=1)
    return jnp.stack([jnp.tile(t, (1, HEADS_PER_TILE)) for t in (c, s_up, s_dn)])


def kernel(x_prompt, x_sample, cache_moba_k, cache_moba_v, cache_sb_k, cache_sb_v, page_table, c_prompt, c_sample,
           ln0_g, ln0_b, w_mod, b_mod, w_in, g_out_a, g_out_b, w_out, ln1_g, ln1_b, w_gate, w_up, w_down,
           ln2_g, ln2_b):
    b, s, d = x_prompt.shape
    db, t, _ = x_sample.shape
    depth = w_mod.shape[0]
    assert depth == 1, "ln0 is fused into the first layer's kernels; only a one-layer trunk is supported"
    wa, wb = g_out_a.shape[1], g_out_b.shape[1]
    ha, hb = wa // HEAD_DIM, wb // HEAD_DIM
    n_pages = page_table.shape[1]
    past = n_pages * PAGE_SIZE
    assert s % MOBA_BLOCK == 0 and past % MOBA_BLOCK == 0
    alpha = (2 * depth) ** 0.25
    vec = lambda a: a.reshape(1, -1)

    n_mod = 6
    rows = b + db
    pad = -rows % 8
    c_all = jnp.concatenate([c_prompt, c_sample, jnp.zeros((pad, d), F32)], axis=0)
    m = _mod_call(c_all, w_mod[0], b_mod[0].reshape(1, -1))
    mods_p = m[:b].reshape(b, n_mod, 1, d)
    mods_s = jnp.repeat(m[b:rows].reshape(db, n_mod, d), t, axis=0).transpose(1, 0, 2)

    w_in_bf = w_in[0].astype(BF16)
    wo_bf, wg_bf, wu_bf, wd_bf = (w[0].astype(BF16) for w in (w_out, w_gate, w_up, w_down))
    g0, b0 = vec(ln0_g), vec(ln0_b)
    ln = [vec(a[0]) for a in (ln1_g, ln1_b, ln2_g, ln2_b)]
    goa, gob = g_out_a[0].reshape(1, wa), g_out_b[0].reshape(1, wb)

    tm = 512
    ff_chunk = 256

    xp2 = x_prompt.reshape(b * s, d)
    mod_spec_p = pl.BlockSpec((None, n_mod, 1, d), lambda i: (i // (s // tm), 0, 0, 0))
    rot_p = _rotary_tables(jnp.arange(s, dtype=jnp.int32))
    (qa, qb, ka, va, kb, vb, kab, vab, kbb, vbb, kmean) = _pre_call(
        xp2, g0, b0, mods_p, mod_spec_p, rot_p, w_in_bf, wa, wb, tm, True)
    mixed_a = _moba_call(qa, kab, vab, kmean.reshape(b, s // MOBA_BLOCK, wa), goa, b, s, n_chain=4)
    mixed_b = _sb_call(qb, kbb, vbb, gob, b, s, n_chain=4)
    y_p = _post_call(xp2, g0, b0, mixed_a, mixed_b, mods_p, mod_spec_p, wo_bf, ln[0], ln[1],
                     wg_bf, wu_bf, wd_bf, ln[2], ln[3], alpha, tm, ff_chunk)

    rs = db * t
    xs2 = x_sample.reshape(rs, d)
    mod_spec_s = pl.BlockSpec((n_mod, rs, d), lambda i: (0, 0, 0))
    rot_s = _rotary_tables(jnp.tile(past + jnp.arange(t, dtype=jnp.int32), db))
    (sqa, sqb, ska, sva, skb, svb) = _pre_call(
        xs2, g0, b0, mods_s, mod_spec_s, rot_s, w_in_bf, wa, wb, rs, False)
    tok3 = lambda a: a.reshape(db, t, a.shape[-1])
    pages = lambda c: c[0].transpose(0, 2, 3, 1).reshape(c.shape[1], -1, PAGE_SIZE)
    smixed_a = _sample_attn_call(_smoba_kernel, "smoba", page_table, tok3(sqa), tok3(ska), tok3(sva), goa,
                                 pages(cache_moba_k), pages(cache_moba_v))
    sb_args = (page_table, tok3(sqb), tok3(skb), tok3(svb), gob, pages(cache_sb_k), pages(cache_sb_v))
    recent_b, log_keep = _ssb_recent_call(*sb_args, n_recent=min(SB_RECENT_PAGES, n_pages), group=SB_GROUP)
    smixed_b = lax.cond(
        jnp.max(log_keep) > UNDERFLOW_LOG,
        lambda: _sample_attn_call(_ssb_kernel, "ssb", *sb_args, scratch_shapes=[
            pltpu.VMEM((t * hb, wb), F32), pltpu.VMEM((t * hb, LANES), F32)]),
        lambda: recent_b)
    y_s = _post_call(xs2, g0, b0, smixed_a.reshape(rs, wa), smixed_b.reshape(rs, wb), mods_s, mod_spec_s,
                     wo_bf, ln[0], ln[1], wg_bf, wu_bf, wd_bf, ln[2], ln[3], alpha, rs, ff_chunk)

    kv_p = lambda a, h: a.reshape(b, h, HEAD_DIM, s).transpose(0, 3, 1, 2)[None]
    return (y_p.reshape(b, s, d), y_s.reshape(db, t, d),
            kv_p(ka, ha), kv_p(va, ha), kv_p(kb, hb), kv_p(vb, hb),
            ska.reshape(1, db, t, ha, HEAD_DIM), sva.reshape(1, db, t, ha, HEAD_DIM),
            skb.reshape(1, db, t, hb, HEAD_DIM), svb.reshape(1, db, t, hb, HEAD_DIM))
```

```python
import functools

import jax
import jax.numpy as jnp
from jax import lax
from jax.experimental import pallas as pl
from jax.experimental.pallas import tpu as pltpu

F32 = jnp.float32
BF16 = jnp.bfloat16

HEAD_DIM = 64
MOBA_BLOCK = 256
MOBA_TOPK = 3
PAGE_SIZE = 128
ROPE_THETA = 500000.0
ROT_DIM = HEAD_DIM // 4
LN_EPS = 1e-5
LANES = 128
HEADS_PER_TILE = LANES // HEAD_DIM
NEG = -1e30
UNDERFLOW_LOG = -110.0
MOBA_SCORE_LEAD = 4
SB_RECENT_PAGES = 2
SB_GROUP = 4
VMEM_LIMIT = 56 * 1024 * 1024


def _dot(a, b):
    return jnp.dot(a, b, preferred_element_type=F32)


def _dot_nt(a, b):
    return lax.dot_general(a, b, (((1,), (1,)), ((), ())), preferred_element_type=F32)


def _dot_tn(a, b):
    return lax.dot_general(a, b, (((0,), (0,)), ((), ())), preferred_element_type=F32)


def _split_bf16(x):
    hi = x.astype(BF16)
    lo = (x - hi.astype(F32)).astype(BF16)
    return hi, lo


def _layer_norm(x, g, b):
    mu = jnp.mean(x, axis=-1, keepdims=True)
    xc = x - mu
    var = jnp.mean(xc * xc, axis=-1, keepdims=True)
    return xc * lax.rsqrt(var + LN_EPS) * g + b


def _softplus(z):
    return jnp.maximum(z, 0.0) + jnp.log(1.0 + jnp.exp(-jnp.abs(z)))


def _later_matrix(n):
    return (lax.broadcasted_iota(jnp.int32, (n, n), 0) > lax.broadcasted_iota(jnp.int32, (n, n), 1)).astype(BF16)


def _params(n_grid):
    return pltpu.CompilerParams(dimension_semantics=("arbitrary",) * n_grid, vmem_limit_bytes=VMEM_LIMIT)


def _const_spec(shape):
    zeros = (0,) * len(shape)
    return pl.BlockSpec(shape, lambda *_: zeros, pipeline_mode=pl.Buffered(1))


def _mod_kernel(c_ref, w_ref, b_ref, o_ref):
    c = c_ref[...]
    a = c / (1.0 + jnp.exp(-c))
    o_ref[...] = _dot(a.astype(BF16), w_ref[...].astype(BF16)) + b_ref[...]


def _mod_call(c_all, w_mod, b_mod, tn=1536):
    r, d = c_all.shape
    n = w_mod.shape[1]
    return pl.pallas_call(
        _mod_kernel,
        out_shape=jax.ShapeDtypeStruct((r, n), F32),
        grid=(n // tn,),
        in_specs=[pl.BlockSpec((r, d), lambda j: (0, 0)),
                  pl.BlockSpec((d, tn), lambda j: (0, j)),
                  pl.BlockSpec((1, tn), lambda j: (0, j))],
        out_specs=pl.BlockSpec((r, tn), lambda j: (0, j)),
        compiler_params=_params(1),
        name="mod",
    )(c_all, w_mod, b_mod)


def _pre_kernel(x_ref, g0_ref, b0_ref, mod_ref, rot_ref, w_ref, qa_ref, qb_ref, ka_ref, va_ref, kb_ref, vb_ref,
                *extra_refs, wa, wb):
    transposed = bool(extra_refs)

    def put(ref, sl, val):
        if transposed:
            ref[sl, :] = val.T
        else:
            ref[:, sl] = val

    xn = _layer_norm(x_ref[...], g0_ref[...], b0_ref[...])
    h = xn * (1.0 + mod_ref[1]) + mod_ref[0]
    qkv = _dot(h.astype(BF16), w_ref[...])
    cos, s_up, s_dn = rot_ref[0], rot_ref[1], rot_ref[2]

    def rotate(t):
        return t * cos + pltpu.roll(t, LANES - ROT_DIM // 2, 1) * s_up + pltpu.roll(t, ROT_DIM // 2, 1) * s_dn

    for t in range(wa // LANES):
        sl = slice(t * LANES, (t + 1) * LANES)
        qa = rotate(qkv[:, sl])
        ka = rotate(qkv[:, wa + t * LANES:wa + (t + 1) * LANES])
        qa_ref[:, sl] = (qa * HEAD_DIM ** -0.5).astype(BF16)
        put(ka_ref, sl, ka)
        if extra_refs:
            kab_ref, km_ref = extra_refs[0], extra_refs[4]
            kab_ref[sl, :] = ka.T.astype(BF16)
            for g in range(km_ref.shape[0]):
                blk = ka[g * MOBA_BLOCK:(g + 1) * MOBA_BLOCK, :]
                km_ref[g, :, sl] = jnp.sum(blk, axis=0, keepdims=True) * (1.0 / MOBA_BLOCK)
    o = 3 * wa
    va, kb, vb = qkv[:, 2 * wa:o], qkv[:, o + wb:o + 2 * wb], qkv[:, o + 2 * wb:o + 3 * wb]
    qb_ref[...] = (qkv[:, o:o + wb] * HEAD_DIM ** -0.5).astype(BF16)
    for ref, val in ((va_ref, va), (kb_ref, kb), (vb_ref, vb)):
        for t in range(val.shape[1] // LANES):
            sl = slice(t * LANES, (t + 1) * LANES)
            put(ref, sl, val[:, sl])
    if extra_refs:
        extra_refs[1][...] = va.astype(BF16)
        for t in range(wb // LANES):
            sl = slice(t * LANES, (t + 1) * LANES)
            extra_refs[2][sl, :] = kb[:, sl].T.astype(BF16)
        extra_refs[3][...] = vb.astype(BF16)


def _pre_call(x2, g0, b0, mods, mod_spec, rot, w_in_bf, wa, wb, tm, prompt):
    r, d = x2.shape
    n_rot = rot.shape[1] // tm
    row = lambda w: pl.BlockSpec((tm, w), lambda i: (i, 0))
    if prompt:
        nt = rot.shape[1] // tm
        kv = lambda w: jax.ShapeDtypeStruct((r // rot.shape[1], w, rot.shape[1]), F32)
        kv_spec = lambda w: pl.BlockSpec((None, w, tm), lambda i: (i // nt, 0, i % nt))
    else:
        kv = lambda w: jax.ShapeDtypeStruct((r, w), F32)
        kv_spec = row
    outs = [jax.ShapeDtypeStruct((r, wa), BF16), jax.ShapeDtypeStruct((r, wb), BF16), kv(wa), kv(wa), kv(wb), kv(wb)]
    out_specs = [row(wa), row(wb), kv_spec(wa), kv_spec(wa), kv_spec(wb), kv_spec(wb)]
    if prompt:
        kt = lambda w: jax.ShapeDtypeStruct((r // rot.shape[1], w, rot.shape[1]), BF16)
        outs += [kt(wa), jax.ShapeDtypeStruct((r, wa), BF16), kt(wb), jax.ShapeDtypeStruct((r, wb), BF16),
                 jax.ShapeDtypeStruct((r // MOBA_BLOCK, 1, wa), F32)]
        out_specs += [kv_spec(wa), row(wa), kv_spec(wb), row(wb),
                      pl.BlockSpec((tm // MOBA_BLOCK, 1, wa), lambda i: (i, 0, 0))]
    return pl.pallas_call(
        functools.partial(_pre_kernel, wa=wa, wb=wb),
        out_shape=outs,
        grid=(r // tm,),
        in_specs=[row(d), _const_spec((1, d)), _const_spec((1, d)), mod_spec,
                  pl.BlockSpec((3, tm, LANES), lambda i: (0, i % n_rot, 0)),
                  _const_spec(w_in_bf.shape)],
        out_specs=out_specs,
        compiler_params=_params(1),
        name="pre",
    )(x2, g0, b0, mods, rot, w_in_bf)


def _merge_heads_and_norm(o2, gain, tq, denominators_in_other_lanes=False):
    lane = lax.broadcasted_iota(jnp.int32, (1, LANES), 1)
    first = lane < HEAD_DIM
    top, bot = o2[:tq], o2[tq:]
    o = jnp.where(first, top, bot)
    if denominators_in_other_lanes:
        o = o / jnp.where(first, pltpu.roll(top, HEAD_DIM, 1), pltpu.roll(bot, HEAD_DIM, 1))
    sq = o * o
    ss0 = jnp.sum(jnp.where(first, sq, 0.0), axis=1, keepdims=True)
    ss1 = jnp.sum(jnp.where(first, 0.0, sq), axis=1, keepdims=True)
    scale = jnp.where(first, lax.rsqrt(ss0 * (1.0 / HEAD_DIM) + LN_EPS), lax.rsqrt(ss1 * (1.0 / HEAD_DIM) + LN_EPS))
    return o * scale * gain


def _moba_kernel(q_ref, k_ref, v_ref, km_ref, go_ref, o_ref, q2_ref, s_ref, acc_ref, m_ref, *,
                 tq, nblk, n_chain):
    i = pl.program_id(2)
    i_f = i.astype(F32)
    lane = lax.broadcasted_iota(jnp.int32, (1, LANES), 1)
    n_idx = lax.broadcasted_iota(jnp.int32, (nblk, tq), 0).astype(F32)
    valid = n_idx < i_f
    place = (lax.broadcasted_iota(jnp.int32, (nblk, LANES), 1)
             == lax.broadcasted_iota(jnp.int32, (nblk, LANES), 0)).astype(BF16)
    gates = []
    for c in range(n_chain):
        tile = slice(c * LANES, (c + 1) * LANES)
        q = q_ref[:, tile]
        for hh in range(HEADS_PER_TILE):
            q2_ref[c, hh * tq:(hh + 1) * tq, :LANES] = jnp.where((lane // HEAD_DIM) == hh, q, jnp.zeros_like(q))
        g2 = _dot_nt(jnp.concatenate(_split_bf16(km_ref[0, :, tile]), axis=0), q2_ref[c, :, :LANES])
        gates.append(g2[:nblk] + g2[nblk:])
    biases = []
    for c in range(n_chain):
        halves = []
        for hh in range(HEADS_PER_TILE):
            g = gates[c][:, hh * tq:(hh + 1) * tq]
            cand = valid
            sel = n_idx == i_f
            for _ in range(MOBA_TOPK):
                gm = jnp.where(cand, g, -jnp.inf)
                best = jnp.max(gm, axis=0, keepdims=True)
                first = jnp.min(jnp.where(cand & (gm == best), n_idx, float(nblk)), axis=0, keepdims=True)
                pick = n_idx == first
                sel = sel | pick
                cand = cand & jnp.logical_not(pick)
            halves.append(jnp.where(sel, 0.0, NEG).astype(BF16))
        biases.append(jnp.concatenate(halves, axis=1))
    for c in range(n_chain):
        q2_ref[c, :, LANES:] = _dot_tn(biases[c], place).astype(BF16)
    row_in_tile = lax.broadcasted_iota(jnp.int32, (2 * tq, 1), 0) & (tq - 1)
    kofs = lax.broadcasted_iota(jnp.int32, (1, MOBA_BLOCK), 1)

    first_head = lane < HEAD_DIM

    acc_ref[...] = jnp.zeros_like(acc_ref)
    m_ref[...] = jnp.full_like(m_ref, NEG)

    block_row = lax.broadcasted_iota(jnp.int32, (LANES, MOBA_BLOCK), 0)

    def score(c, j):
        kt = k_ref[c * LANES:(c + 1) * LANES, pl.ds(pl.multiple_of(j * MOBA_BLOCK, MOBA_BLOCK), MOBA_BLOCK)]
        return _dot(q2_ref[c], jnp.concatenate([kt, (block_row == j).astype(BF16)], axis=0))

    lead = s_ref.shape[0]
    for c in range(lead):
        s_ref[c] = score(c, 0)

    def step(j, causal):
        start = pl.multiple_of(j * MOBA_BLOCK, MOBA_BLOCK)
        scores = [s_ref[c] for c in range(lead)]
        for c in range(n_chain):
            if c + lead < n_chain:
                scores.append(score(c + lead, j))
            elif not causal:
                s_ref[c + lead - n_chain] = score(c + lead - n_chain, j + 1)
            vs = v_ref[pl.ds(start, MOBA_BLOCK), c * LANES:(c + 1) * LANES]
            s = scores[c]
            if causal:
                s = jnp.where(kofs <= row_in_tile, s, NEG)
            m_prev = m_ref[c]
            m_new = jnp.maximum(m_prev, jnp.max(s, axis=1, keepdims=True))
            alpha = jnp.exp(m_prev - m_new)
            p = jnp.exp(s - jnp.concatenate([m_new] * (MOBA_BLOCK // LANES), axis=1)).astype(BF16)
            ones = jnp.ones_like(vs)
            pv = [_dot(p[:tq], jnp.where(first_head, vs, ones)), _dot(p[tq:], jnp.where(first_head, ones, vs))]
            acc_ref[c] = alpha * acc_ref[c] + jnp.concatenate(pv, axis=0)
            m_ref[c] = m_new

    def past_step(j, carry):
        step(j, False)
        return carry

    lax.fori_loop(0, i, past_step, 0)
    step(i, True)
    for c in range(n_chain):
        tile = slice(c * LANES, (c + 1) * LANES)
        o_ref[:, tile] = _merge_heads_and_norm(acc_ref[c], go_ref[:, tile], tq, True).astype(o_ref.dtype)


def _moba_call(qa, kab, vab, kmean, g_out, b, s, n_chain):
    tq = MOBA_BLOCK
    nq = s // tq
    nblk = s // MOBA_BLOCK
    w = n_chain * LANES
    n_groups = qa.shape[1] // w
    stats = pltpu.VMEM((n_chain, 2 * tq, LANES), F32)
    return pl.pallas_call(
        functools.partial(_moba_kernel, tq=tq, nblk=nblk, n_chain=n_chain),
        out_shape=jax.ShapeDtypeStruct(qa.shape, BF16),
        grid=(b, n_groups, nq),
        in_specs=[pl.BlockSpec((tq, w), lambda bi, hp, i: (bi * nq + i, hp)),
                  pl.BlockSpec((None, w, s), lambda bi, hp, i: (bi, hp, 0)),
                  pl.BlockSpec((s, w), lambda bi, hp, i: (bi, hp)),
                  pl.BlockSpec((1, nblk, w), lambda bi, hp, i: (bi, 0, hp)),
                  pl.BlockSpec((1, w), lambda bi, hp, i: (0, hp))],
        out_specs=pl.BlockSpec((tq, w), lambda bi, hp, i: (bi * nq + i, hp)),
        scratch_shapes=[pltpu.VMEM((n_chain, 2 * tq, 2 * LANES), BF16),
                        pltpu.VMEM((MOBA_SCORE_LEAD, 2 * tq, MOBA_BLOCK), F32), stats, stats],
        compiler_params=_params(3),
        name="moba",
    )(qa, kab, vab, kmean, g_out)


def _sb_kernel(q_ref, k_ref, v_ref, go_ref, o_ref, q2_ref, acc_ref, car_ref, *, tq, n_chain):
    i = pl.program_id(2)
    lane = lax.broadcasted_iota(jnp.int32, (1, LANES), 1)
    for c in range(n_chain):
        q = q_ref[:, c * LANES:(c + 1) * LANES]
        for hh in range(HEADS_PER_TILE):
            q2_ref[c, hh * tq:(hh + 1) * tq, :] = jnp.where((lane // HEAD_DIM) == hh, q, jnp.zeros_like(q))
    row_in_tile = lax.broadcasted_iota(jnp.int32, (2 * tq, 1), 0) & (tq - 1)
    kofs = lax.broadcasted_iota(jnp.int32, (1, tq), 1)
    later = _later_matrix(tq)

    def step(start, tk, first):
        vis = kofs < row_in_tile
        zs = [_dot(q2_ref[c], k_ref[c * LANES:(c + 1) * LANES, pl.ds(start, tk)]) for c in range(n_chain)]
        mids = []
        for c in range(n_chain):
            sp = _softplus(zs[c])
            spk = jnp.where(vis, sp, 0.0) if first else sp
            mids.append((zs[c] - sp, _dot(spk.astype(BF16), later[:tk, :tk]), jnp.sum(spk, axis=1, keepdims=True)))
        for c in range(n_chain):
            log_beta, after, drop = mids[c]
            w = jnp.exp(log_beta - after)
            if first:
                w = jnp.where(vis, w, 0.0)
            pv = _dot(w.astype(BF16), v_ref[pl.ds(start, tk), c * LANES:(c + 1) * LANES])
            if first:
                acc_ref[c] = pv
                car_ref[c] = jnp.broadcast_to(-drop, pv.shape)
            else:
                car = car_ref[c]
                acc_ref[c] += jnp.exp(car) * pv
                car_ref[c] = car - drop

    step(pl.multiple_of(i * tq, tq), tq, True)

    def cond(state):
        t, car_max = state
        return jnp.logical_and(t < i, car_max > UNDERFLOW_LOG)

    def body(state):
        t, _ = state
        step(pl.multiple_of((i - 1 - t) * tq, tq), tq, False)
        return t + 1, jnp.max(car_ref[...])

    lax.while_loop(cond, body, (jnp.int32(0), jnp.max(car_ref[...])))
    for c in range(n_chain):
        tile = slice(c * LANES, (c + 1) * LANES)
        o_ref[:, tile] = _merge_heads_and_norm(acc_ref[c], go_ref[:, tile], tq).astype(o_ref.dtype)


def _sb_call(qb, kbb, vbb, g_out, b, s, n_chain):
    tq = MOBA_BLOCK
    nq = s // tq
    w = n_chain * LANES
    n_groups = qb.shape[1] // w
    stats = pltpu.VMEM((n_chain, 2 * tq, LANES), F32)
    return pl.pallas_call(
        functools.partial(_sb_kernel, tq=tq, n_chain=n_chain),
        out_shape=jax.ShapeDtypeStruct(qb.shape, BF16),
        grid=(b, n_groups, nq),
        in_specs=[pl.BlockSpec((tq, w), lambda bi, hp, i: (bi * nq + i, hp)),
                  pl.BlockSpec((None, w, s), lambda bi, hp, i: (bi, hp, 0)),
                  pl.BlockSpec((s, w), lambda bi, hp, i: (bi, hp)),
                  pl.BlockSpec((1, w), lambda bi, hp, i: (0, hp))],
        out_specs=pl.BlockSpec((tq, w), lambda bi, hp, i: (bi * nq + i, hp)),
        scratch_shapes=[pltpu.VMEM((n_chain, 2 * tq, LANES), BF16), stats, stats],
        compiler_params=_params(3),
        name="sb",
    )(qb, kbb, vbb, g_out)


def _block_diag_queries(q, n_heads):
    t, w = q.shape
    head_of_lane = lax.broadcasted_iota(jnp.int32, (n_heads, w), 1) // HEAD_DIM
    hmask = head_of_lane == lax.broadcasted_iota(jnp.int32, (n_heads, w), 0)
    rows = [jnp.where(hmask, jnp.broadcast_to(q[ti:ti + 1, :], (n_heads, w)), 0.0) for ti in range(t)]
    return jnp.concatenate(rows, axis=0), jnp.concatenate([hmask] * t, axis=0)


def _collapse_heads(o, hmask_rows, gain, n_tok, n_heads):
    oh = jnp.where(hmask_rows, o, 0.0)
    ss = jnp.sum(oh * oh, axis=1, keepdims=True) * (1.0 / HEAD_DIM)
    of = oh * lax.rsqrt(ss + LN_EPS)
    rows = [jnp.sum(of[ti * n_heads:(ti + 1) * n_heads, :], axis=0, keepdims=True) for ti in range(n_tok)]
    return jnp.concatenate(rows, axis=0) * gain


def _smoba_kernel(pt_ref, q_ref, kn_ref, vn_ref, go_ref, *rest, n_pages, n_tok, n_heads):
    del pt_ref
    k_pages, v_pages, o_ref = rest[:n_pages], rest[n_pages:2 * n_pages], rest[2 * n_pages]
    qbd, hmask_rows = _block_diag_queries(q_ref[0].astype(F32), n_heads)
    qbd_bf = qbd.astype(BF16)
    kn, vn = kn_ref[0], vn_ref[0]
    tok_of_row = lax.broadcasted_iota(jnp.int32, (n_tok * n_heads, 1), 0) // n_heads
    pages_per_block = MOBA_BLOCK // PAGE_SIZE
    n_blocks = n_pages // pages_per_block
    s_pages = [_dot(qbd_bf, k_pages[p][...].astype(BF16)) for p in range(n_pages)]
    gates = [sum(jnp.sum(s_pages[n * pages_per_block + u], axis=1, keepdims=True) for u in range(pages_per_block))
             for n in range(n_blocks)]
    biases = []
    for n in range(n_blocks):
        rank = jnp.zeros_like(gates[n])
        for n2 in range(n_blocks):
            if n2 != n:
                beats = (gates[n2] > gates[n]) | ((gates[n2] == gates[n]) & (n2 < n))
                rank = rank + beats.astype(F32)
        biases.append(jnp.where(rank < min(MOBA_TOPK, n_blocks), 0.0, NEG))
    s_new = [jnp.where(t <= tok_of_row, jnp.sum(qbd * kn[t:t + 1, :], axis=1, keepdims=True), NEG)
             for t in range(n_tok)]
    s_sel = [s_pages[p] + biases[p // pages_per_block] for p in range(n_pages)]
    m = functools.reduce(jnp.maximum, s_new)
    for p in range(n_pages):
        m = jnp.maximum(m, jnp.max(s_sel[p], axis=1, keepdims=True))
    acc = jnp.zeros(qbd.shape, F32)
    l = jnp.zeros_like(m)
    for t in range(n_tok):
        pt = jnp.exp(s_new[t] - m)
        l = l + pt
        acc = acc + pt * vn[t:t + 1, :]
    for p in range(n_pages):
        pp = jnp.exp(s_sel[p] - m)
        l = l + jnp.sum(pp, axis=1, keepdims=True)
        acc = acc + _dot_nt(pp.astype(BF16), v_pages[p][...].astype(BF16))
    o_ref[0] = _collapse_heads(acc / l, hmask_rows, go_ref[...], n_tok, n_heads).astype(o_ref.dtype)


def _sb_new_tokens(qbd, kn, vn, n_tok, n_heads):
    tok_of_row = lax.broadcasted_iota(jnp.int32, (n_tok * n_heads, 1), 0) // n_heads
    acc = jnp.zeros(qbd.shape, F32)
    carry = jnp.zeros((n_tok * n_heads, 1), F32)
    for t in reversed(range(n_tok)):
        z = jnp.sum(qbd * kn[t:t + 1, :], axis=1, keepdims=True)
        sp = _softplus(z)
        vis = t < tok_of_row
        acc = acc + jnp.where(vis, jnp.exp(z - sp + carry), 0.0) * vn[t:t + 1, :]
        carry = carry - jnp.where(vis, sp, 0.0)
    return acc, carry


def _ssb_recent_kernel(pt_ref, q_ref, kn_ref, vn_ref, go_ref, *rest, n_recent, group, n_tok, n_heads):
    del pt_ref
    n_in = group * n_recent
    k_pages, v_pages = rest[:n_in], rest[n_in:2 * n_in]
    o_ref, car_ref = rest[2 * n_in:]
    later = _later_matrix(n_recent * PAGE_SIZE)
    gather = lambda pages, g: jnp.concatenate([pages[g * n_recent + u][...] for u in range(n_recent)],
                                              axis=1).astype(BF16)
    state = []
    for g in range(group):
        qbd, hmask_rows = _block_diag_queries(q_ref[g].astype(F32), n_heads)
        acc, carry = _sb_new_tokens(qbd, kn_ref[g], vn_ref[g], n_tok, n_heads)
        state.append((hmask_rows, acc, carry, _dot(qbd.astype(BF16), gather(k_pages, g))))
    mids = []
    for hmask_rows, acc, carry, z in state:
        sp = _softplus(z)
        mids.append((z - sp, _dot(sp.astype(BF16), later), jnp.sum(sp, axis=1, keepdims=True)))
    for g in range(group):
        hmask_rows, acc, carry, _ = state[g]
        log_beta, after, drop = mids[g]
        w = jnp.exp(log_beta - after)
        acc = acc + jnp.exp(carry) * _dot_nt(w.astype(BF16), gather(v_pages, g))
        o_ref[g] = _collapse_heads(acc, hmask_rows, go_ref[...], n_tok, n_heads).astype(o_ref.dtype)
        car_ref[g] = jnp.broadcast_to(jnp.max(carry - drop, axis=0, keepdims=True), car_ref.shape[1:])


def _ssb_recent_call(page_table, q, k_new, v_new, g_out, cache_k, cache_v, n_recent, group):
    db, n_pages = page_table.shape
    _, t, w = q.shape
    tok = lambda: pl.BlockSpec((group, t, w), lambda bi, pt: (bi, 0, 0))

    def page(g, u):
        return pl.BlockSpec((None, w, PAGE_SIZE),
                            lambda bi, pt: (pt[bi * group + g, n_pages - n_recent + u], 0, 0))

    pages = [page(g, u) for g in range(group) for u in range(n_recent)]
    grid_spec = pltpu.PrefetchScalarGridSpec(
        num_scalar_prefetch=1,
        grid=(db // group,),
        in_specs=[tok(), tok(), tok(), pl.BlockSpec((1, w), lambda bi, pt: (0, 0))] + pages * 2,
        out_specs=[tok(), pl.BlockSpec((group, 1, LANES), lambda bi, pt: (bi, 0, 0))],
    )
    return pl.pallas_call(
        functools.partial(_ssb_recent_kernel, n_recent=n_recent, group=group, n_tok=t, n_heads=w // HEAD_DIM),
        out_shape=[jax.ShapeDtypeStruct(q.shape, BF16), jax.ShapeDtypeStruct((db, 1, LANES), F32)],
        grid_spec=grid_spec,
        compiler_params=_params(1),
        name="ssb_recent",
    )(page_table, q, k_new, v_new, g_out, *([cache_k] * len(pages)), *([cache_v] * len(pages)))


def _ssb_kernel(pt_ref, q_ref, kn_ref, vn_ref, go_ref, *rest, n_pages, n_tok, n_heads):
    del pt_ref
    k_pages, v_pages = rest[:n_pages], rest[n_pages:2 * n_pages]
    o_ref, acc_ref, car_ref = rest[2 * n_pages:]
    qbd, hmask_rows = _block_diag_queries(q_ref[0].astype(F32), n_heads)
    qbd_bf = qbd.astype(BF16)
    later = _later_matrix(PAGE_SIZE)
    acc, carry = _sb_new_tokens(qbd, kn_ref[0], vn_ref[0], n_tok, n_heads)
    acc_ref[...] = acc
    car_ref[...] = jnp.broadcast_to(carry, car_ref.shape)
    for p in reversed(range(n_pages)):
        @pl.when(jnp.max(car_ref[...]) > UNDERFLOW_LOG)
        def _():
            z = _dot(qbd_bf, k_pages[p][...].astype(BF16))
            sp = _softplus(z)
            w = jnp.exp(z - sp - _dot(sp.astype(BF16), later))
            car = car_ref[...]
            acc_ref[...] += jnp.exp(car[:, :1]) * _dot_nt(w.astype(BF16), v_pages[p][...].astype(BF16))
            car_ref[...] = car - jnp.sum(sp, axis=1, keepdims=True)
    o_ref[0] = _collapse_heads(acc_ref[...], hmask_rows, go_ref[...], n_tok, n_heads).astype(o_ref.dtype)


def _sample_attn_call(body, name, page_table, q, k_new, v_new, g_out, cache_k, cache_v, scratch_shapes=()):
    db, n_pages = page_table.shape
    _, t, w = q.shape
    n_heads = w // HEAD_DIM
    tok = lambda: pl.BlockSpec((1, t, w), lambda bi, pt: (bi, 0, 0))
    page = lambda p: pl.BlockSpec((None, w, PAGE_SIZE), lambda bi, pt: (pt[bi, p], 0, 0))
    grid_spec = pltpu.PrefetchScalarGridSpec(
        num_scalar_prefetch=1,
        grid=(db,),
        in_specs=[tok(), tok(), tok(), pl.BlockSpec((1, w), lambda bi, pt: (0, 0))]
                 + [page(p) for p in range(n_pages)] * 2,
        out_specs=tok(),
        scratch_shapes=scratch_shapes,
    )
    return pl.pallas_call(
        functools.partial(body, n_pages=n_pages, n_tok=t, n_heads=n_heads),
        out_shape=jax.ShapeDtypeStruct(q.shape, BF16),
        grid_spec=grid_spec,
        compiler_params=_params(1),
        name=name,
    )(page_table, q, k_new, v_new, g_out, *([cache_k] * n_pages), *([cache_v] * n_pages))


def _post_kernel(x_ref, g0_ref, b0_ref, ma_ref, mb_ref, mod_ref, wo_ref, g1_ref, b1_ref,
                 wg_ref, wu_ref, wd_ref, g2_ref, b2_ref, o_ref, act_ref, *, alpha, wa, ff_chunk):
    xn = _layer_norm(x_ref[...], g0_ref[...], b0_ref[...])
    attn = _dot(ma_ref[...], wo_ref[:wa, :]) + _dot(mb_ref[...], wo_ref[wa:, :])
    x1 = _layer_norm(alpha * xn + mod_ref[2] * attn, g1_ref[...], b1_ref[...])
    hb = (x1 * (1.0 + mod_ref[4]) + mod_ref[3]).astype(BF16)
    for c in range(0, wg_ref.shape[1], ff_chunk):
        gate = _dot(hb, wg_ref[:, c:c + ff_chunk])
        up = _dot(hb, wu_ref[:, c:c + ff_chunk])
        act_ref[:, c:c + ff_chunk] = (gate / (1.0 + jnp.exp(-gate)) * up).astype(BF16)
    f = _dot(act_ref[...], wd_ref[...])
    o_ref[...] = _layer_norm(alpha * x1 + mod_ref[5] * f, g2_ref[...], b2_ref[...])


def _post_call(x2, g0, b0, ma, mb, mods, mod_spec, wo, g1, b1, wg, wu, wd, g2, b2, alpha, tm, ff_chunk):
    r, d = x2.shape
    wa = ma.shape[1]
    row = lambda w: pl.BlockSpec((tm, w), lambda i: (i, 0))
    vec = lambda: _const_spec((1, d))
    return pl.pallas_call(
        functools.partial(_post_kernel, alpha=alpha, wa=wa, ff_chunk=ff_chunk),
        out_shape=jax.ShapeDtypeStruct((r, d), F32),
        grid=(r // tm,),
        in_specs=[row(d), vec(), vec(), row(wa), row(mb.shape[1]), mod_spec, _const_spec(wo.shape), vec(), vec(),
                  _const_spec(wg.shape), _const_spec(wu.shape), _const_spec(wd.shape), vec(), vec()],
        out_specs=row(d),
        scratch_shapes=[pltpu.VMEM((tm, wg.shape[1]), BF16)],
        compiler_params=_params(1),
        name="post",
    )(x2, g0, b0, ma, mb, mods, wo, g1, b1, wg, wu, wd, g2, b2)


def _rotary_tables(pos):
    half = ROT_DIM // 2
    inv = ROPE_THETA ** (-jnp.arange(half, dtype=F32) * 2.0 / ROT_DIM)
    ang = pos.astype(F32)[:, None] * inv
    cos, sin = jnp.cos(ang), jnp.sin(ang)
    n = pos.shape[0]
    rest = HEAD_DIM - ROT_DIM
    c = jnp.concatenate([cos, cos, jnp.ones((n, rest), F32)], axis=1)
    s_up = jnp.concatenate([-sin, jnp.zeros((n, half + rest), F32)], axis=1)
    s_dn = jnp.concatenate([jnp.zeros((n, half), F32), sin, jnp.zeros((n, rest), F32)], axis=1)
    return jnp.stack([jnp.tile(t, (1, HEADS_PER_TILE)) for t in (c, s_up, s_dn)])


def kernel(x_prompt, x_sample, cache_moba_k, cache_moba_v, cache_sb_k, cache_sb_v, page_table, c_prompt, c_sample,
           ln0_g, ln0_b, w_mod, b_mod, w_in, g_out_a, g_out_b, w_out, ln1_g, ln1_b, w_gate, w_up, w_down,
           ln2_g, ln2_b):
    b, s, d = x_prompt.shape
    db, t, _ = x_sample.shape
    depth = w_mod.shape[0]
    assert depth == 1, "ln0 is fused into the first layer's kernels; only a one-layer trunk is supported"
    wa, wb = g_out_a.shape[1], g_out_b.shape[1]
    ha, hb = wa // HEAD_DIM, wb // HEAD_DIM
    n_pages = page_table.shape[1]
    past = n_pages * PAGE_SIZE
    assert s % MOBA_BLOCK == 0 and past % MOBA_BLOCK == 0
    alpha = (2 * depth) ** 0.25
    vec = lambda a: a.reshape(1, -1)

    n_mod = 6
    rows = b + db
    pad = -rows % 8
    c_all = jnp.concatenate([c_prompt, c_sample, jnp.zeros((pad, d), F32)], axis=0)
    m = _mod_call(c_all, w_mod[0], b_mod[0].reshape(1, -1))
    mods_p = m[:b].reshape(b, n_mod, 1, d)
    mods_s = jnp.repeat(m[b:rows].reshape(db, n_mod, d), t, axis=0).transpose(1, 0, 2)

    w_in_bf = w_in[0].astype(BF16)
    wo_bf, wg_bf, wu_bf, wd_bf = (w[0].astype(BF16) for w in (w_out, w_gate, w_up, w_down))
    g0, b0 = vec(ln0_g), vec(ln0_b)
    ln = [vec(a[0]) for a in (ln1_g, ln1_b, ln2_g, ln2_b)]
    goa, gob = g_out_a[0].reshape(1, wa), g_out_b[0].reshape(1, wb)

    tm = 512
    ff_chunk = 256

    xp2 = x_prompt.reshape(b * s, d)
    mod_spec_p = pl.BlockSpec((None, n_mod, 1, d), lambda i: (i // (s // tm), 0, 0, 0))
    rot_p = _rotary_tables(jnp.arange(s, dtype=jnp.int32))
    (qa, qb, ka, va, kb, vb, kab, vab, kbb, vbb, kmean) = _pre_call(
        xp2, g0, b0, mods_p, mod_spec_p, rot_p, w_in_bf, wa, wb, tm, True)
    mixed_a = _moba_call(qa, kab, vab, kmean.reshape(b, s // MOBA_BLOCK, wa), goa, b, s, n_chain=4)
    mixed_b = _sb_call(qb, kbb, vbb, gob, b, s, n_chain=4)
    y_p = _post_call(xp2, g0, b0, mixed_a, mixed_b, mods_p, mod_spec_p, wo_bf, ln[0], ln[1],
                     wg_bf, wu_bf, wd_bf, ln[2], ln[3], alpha, tm, ff_chunk)

    rs = db * t
    xs2 = x_sample.reshape(rs, d)
    mod_spec_s = pl.BlockSpec((n_mod, rs, d), lambda i: (0, 0, 0))
    rot_s = _rotary_tables(jnp.tile(past + jnp.arange(t, dtype=jnp.int32), db))
    (sqa, sqb, ska, sva, skb, svb) = _pre_call(
        xs2, g0, b0, mods_s, mod_spec_s, rot_s, w_in_bf, wa, wb, rs, False)
    tok3 = lambda a: a.reshape(db, t, a.shape[-1])
    pages = lambda c: c[0].transpose(0, 2, 3, 1).reshape(c.shape[1], -1, PAGE_SIZE)
    smixed_a = _sample_attn_call(_smoba_kernel, "smoba", page_table, tok3(sqa), tok3(ska), tok3(sva), goa,
                                 pages(cache_moba_k), pages(cache_moba_v))
    sb_args = (page_table, tok3(sqb), tok3(skb), tok3(svb), gob, pages(cache_sb_k), pages(cache_sb_v))
    recent_b, log_keep = _ssb_recent_call(*sb_args, n_recent=min(SB_RECENT_PAGES, n_pages), group=SB_GROUP)
    smixed_b = lax.cond(
        jnp.max(log_keep) > UNDERFLOW_LOG,
        lambda: _sample_attn_call(_ssb_kernel, "ssb", *sb_args, scratch_shapes=[
            pltpu.VMEM((t * hb, wb), F32), pltpu.VMEM((t * hb, LANES), F32)]),
        lambda: recent_b)
    y_s = _post_call(xs2, g0, b0, smixed_a.reshape(rs, wa), smixed_b.reshape(rs, wb), mods_s, mod_spec_s,
                     wo_bf, ln[0], ln[1], wg_bf, wu_bf, wd_bf, ln[2], ln[3], alpha, rs, ff_chunk)

    kv_p = lambda a, h: a.reshape(b, h, HEAD_DIM, s).transpose(0, 3, 1, 2)[None]
    return (y_p.reshape(b, s, d), y_s.reshape(db, t, d),
            kv_p(ka, ha), kv_p(va, ha), kv_p(kb, hb), kv_p(vb, hb),
            ska.reshape(1, db, t, ha, HEAD_DIM), sva.reshape(1, db, t, ha, HEAD_DIM),
            skb.reshape(1, db, t, hb, HEAD_DIM), svb.reshape(1, db, t, hb, HEAD_DIM))
```

```python
import functools

import jax
import jax.numpy as jnp
from jax import lax
from jax.experimental import pallas as pl
from jax.experimental.pallas import tpu as pltpu

F32 = jnp.float32
BF16 = jnp.bfloat16

HEAD_DIM = 64
MOBA_BLOCK = 256
MOBA_TOPK = 3
PAGE_SIZE = 128
ROPE_THETA = 500000.0
ROT_DIM = HEAD_DIM // 4
LN_EPS = 1e-5
LANES = 128
HEADS_PER_TILE = LANES // HEAD_DIM
NEG = -1e30
UNDERFLOW_LOG = -110.0
MOBA_SCORE_LEAD = 4
SMOBA_GROUP = 2
SB_RECENT_PAGES = 2
SB_GROUP = 4
VMEM_LIMIT = 56 * 1024 * 1024


def _dot(a, b):
    return jnp.dot(a, b, preferred_element_type=F32)


def _dot_nt(a, b):
    return lax.dot_general(a, b, (((1,), (1,)), ((), ())), preferred_element_type=F32)


def _dot_tn(a, b):
    return lax.dot_general(a, b, (((0,), (0,)), ((), ())), preferred_element_type=F32)


def _split_bf16(x):
    hi = x.astype(BF16)
    lo = (x - hi.astype(F32)).astype(BF16)
    return hi, lo


def _layer_norm(x, g, b):
    mu = jnp.mean(x, axis=-1, keepdims=True)
    xc = x - mu
    var = jnp.mean(xc * xc, axis=-1, keepdims=True)
    return xc * lax.rsqrt(var + LN_EPS) * g + b


def _softplus(z):
    return jnp.maximum(z, 0.0) + jnp.log(1.0 + jnp.exp(-jnp.abs(z)))


def _later_matrix(n):
    return (lax.broadcasted_iota(jnp.int32, (n, n), 0) > lax.broadcasted_iota(jnp.int32, (n, n), 1)).astype(BF16)


def _params(n_grid):
    return pltpu.CompilerParams(dimension_semantics=("arbitrary",) * n_grid, vmem_limit_bytes=VMEM_LIMIT)


def _const_spec(shape):
    zeros = (0,) * len(shape)
    return pl.BlockSpec(shape, lambda *_: zeros, pipeline_mode=pl.Buffered(1))


def _mod_kernel(c_ref, w_ref, b_ref, o_ref):
    c = c_ref[...]
    a = c / (1.0 + jnp.exp(-c))
    o_ref[...] = _dot(a.astype(BF16), w_ref[...].astype(BF16)) + b_ref[...]


def _mod_call(c_all, w_mod, b_mod, tn=1536):
    r, d = c_all.shape
    n = w_mod.shape[1]
    return pl.pallas_call(
        _mod_kernel,
        out_shape=jax.ShapeDtypeStruct((r, n), F32),
        grid=(n // tn,),
        in_specs=[pl.BlockSpec((r, d), lambda j: (0, 0)),
                  pl.BlockSpec((d, tn), lambda j: (0, j)),
                  pl.BlockSpec((1, tn), lambda j: (0, j))],
        out_specs=pl.BlockSpec((r, tn), lambda j: (0, j)),
        compiler_params=_params(1),
        name="mod",
    )(c_all, w_mod, b_mod)


def _pre_kernel(x_ref, g0_ref, b0_ref, mod_ref, rot_ref, w_ref, qa_ref, qb_ref, ka_ref, va_ref, kb_ref, vb_ref,
                *extra_refs, wa, wb):
    transposed = bool(extra_refs)

    def put(ref, sl, val):
        if transposed:
            ref[sl, :] = val.T
        else:
            ref[:, sl] = val

    xn = _layer_norm(x_ref[...], g0_ref[...], b0_ref[...])
    h = xn * (1.0 + mod_ref[1]) + mod_ref[0]
    qkv = _dot(h.astype(BF16), w_ref[...])
    cos, s_up, s_dn = rot_ref[0], rot_ref[1], rot_ref[2]

    def rotate(t):
        return t * cos + pltpu.roll(t, LANES - ROT_DIM // 2, 1) * s_up + pltpu.roll(t, ROT_DIM // 2, 1) * s_dn

    for t in range(wa // LANES):
        sl = slice(t * LANES, (t + 1) * LANES)
        qa = rotate(qkv[:, sl])
        ka = rotate(qkv[:, wa + t * LANES:wa + (t + 1) * LANES])
        qa_ref[:, sl] = (qa * HEAD_DIM ** -0.5).astype(BF16)
        put(ka_ref, sl, ka)
        if extra_refs:
            kab_ref, km_ref = extra_refs[0], extra_refs[4]
            kab_ref[sl, :] = ka.T.astype(BF16)
            for g in range(km_ref.shape[0]):
                blk = ka[g * MOBA_BLOCK:(g + 1) * MOBA_BLOCK, :]
                km_ref[g, :, sl] = jnp.sum(blk, axis=0, keepdims=True) * (1.0 / MOBA_BLOCK)
    o = 3 * wa
    va, kb, vb = qkv[:, 2 * wa:o], qkv[:, o + wb:o + 2 * wb], qkv[:, o + 2 * wb:o + 3 * wb]
    qb_ref[...] = (qkv[:, o:o + wb] * HEAD_DIM ** -0.5).astype(BF16)
    for ref, val in ((va_ref, va), (kb_ref, kb), (vb_ref, vb)):
        for t in range(val.shape[1] // LANES):
            sl = slice(t * LANES, (t + 1) * LANES)
            put(ref, sl, val[:, sl])
    if extra_refs:
        extra_refs[1][...] = va.astype(BF16)
        for t in range(wb // LANES):
            sl = slice(t * LANES, (t + 1) * LANES)
            extra_refs[2][sl, :] = kb[:, sl].T.astype(BF16)
        extra_refs[3][...] = vb.astype(BF16)


def _pre_call(x2, g0, b0, mods, mod_spec, rot, w_in_bf, wa, wb, tm, prompt):
    r, d = x2.shape
    n_rot = rot.shape[1] // tm
    row = lambda w: pl.BlockSpec((tm, w), lambda i: (i, 0))
    if prompt:
        nt = rot.shape[1] // tm
        kv = lambda w: jax.ShapeDtypeStruct((r // rot.shape[1], w, rot.shape[1]), F32)
        kv_spec = lambda w: pl.BlockSpec((None, w, tm), lambda i: (i // nt, 0, i % nt))
    else:
        kv = lambda w: jax.ShapeDtypeStruct((r, w), F32)
        kv_spec = row
    outs = [jax.ShapeDtypeStruct((r, wa), BF16), jax.ShapeDtypeStruct((r, wb), BF16), kv(wa), kv(wa), kv(wb), kv(wb)]
    out_specs = [row(wa), row(wb), kv_spec(wa), kv_spec(wa), kv_spec(wb), kv_spec(wb)]
    if prompt:
        kt = lambda w: jax.ShapeDtypeStruct((r // rot.shape[1], w, rot.shape[1]), BF16)
        outs += [kt(wa), jax.ShapeDtypeStruct((r, wa), BF16), kt(wb), jax.ShapeDtypeStruct((r, wb), BF16),
                 jax.ShapeDtypeStruct((r // MOBA_BLOCK, 1, wa), F32)]
        out_specs += [kv_spec(wa), row(wa), kv_spec(wb), row(wb),
                      pl.BlockSpec((tm // MOBA_BLOCK, 1, wa), lambda i: (i, 0, 0))]
    return pl.pallas_call(
        functools.partial(_pre_kernel, wa=wa, wb=wb),
        out_shape=outs,
        grid=(r // tm,),
        in_specs=[row(d), _const_spec((1, d)), _const_spec((1, d)), mod_spec,
                  pl.BlockSpec((3, tm, LANES), lambda i: (0, i % n_rot, 0)),
                  _const_spec(w_in_bf.shape)],
        out_specs=out_specs,
        compiler_params=_params(1),
        name="pre",
    )(x2, g0, b0, mods, rot, w_in_bf)


def _merge_heads_and_norm(o2, gain, tq, denominators_in_other_lanes=False):
    lane = lax.broadcasted_iota(jnp.int32, (1, LANES), 1)
    first = lane < HEAD_DIM
    top, bot = o2[:tq], o2[tq:]
    o = jnp.where(first, top, bot)
    if denominators_in_other_lanes:
        o = o / jnp.where(first, pltpu.roll(top, HEAD_DIM, 1), pltpu.roll(bot, HEAD_DIM, 1))
    sq = o * o
    ss0 = jnp.sum(jnp.where(first, sq, 0.0), axis=1, keepdims=True)
    ss1 = jnp.sum(jnp.where(first, 0.0, sq), axis=1, keepdims=True)
    scale = jnp.where(first, lax.rsqrt(ss0 * (1.0 / HEAD_DIM) + LN_EPS), lax.rsqrt(ss1 * (1.0 / HEAD_DIM) + LN_EPS))
    return o * scale * gain


def _moba_kernel(q_ref, k_ref, v_ref, km_ref, go_ref, o_ref, q2_ref, s_ref, acc_ref, m_ref, *,
                 tq, nblk, n_chain):
    i = pl.program_id(2)
    n_tile = q_ref.shape[2] // LANES

    def chain(c):
        t = c % n_tile
        return c // n_tile, slice(t * LANES, (t + 1) * LANES)

    i_f = i.astype(F32)
    lane = lax.broadcasted_iota(jnp.int32, (1, LANES), 1)
    n_idx = lax.broadcasted_iota(jnp.int32, (nblk, tq), 0).astype(F32)
    valid = n_idx < i_f
    place = (lax.broadcasted_iota(jnp.int32, (nblk, LANES), 1)
             == lax.broadcasted_iota(jnp.int32, (nblk, LANES), 0)).astype(BF16)
    gates = []
    for c in range(n_chain):
        bb, tile = chain(c)
        q = q_ref[bb, :, tile]
        for hh in range(HEADS_PER_TILE):
            q2_ref[c, hh * tq:(hh + 1) * tq, :LANES] = jnp.where((lane // HEAD_DIM) == hh, q, jnp.zeros_like(q))
        g2 = _dot_nt(jnp.concatenate(_split_bf16(km_ref[bb, :, tile]), axis=0), q2_ref[c, :, :LANES])
        gates.append(g2[:nblk] + g2[nblk:])
    biases = []
    for c in range(n_chain):
        halves = []
        for hh in range(HEADS_PER_TILE):
            g = gates[c][:, hh * tq:(hh + 1) * tq]
            cand = valid
            sel = n_idx == i_f
            for _ in range(MOBA_TOPK):
                gm = jnp.where(cand, g, -jnp.inf)
                best = jnp.max(gm, axis=0, keepdims=True)
                first = jnp.min(jnp.where(cand & (gm == best), n_idx, float(nblk)), axis=0, keepdims=True)
                pick = n_idx == first
                sel = sel | pick
                cand = cand & jnp.logical_not(pick)
            halves.append(jnp.where(sel, 0.0, NEG).astype(BF16))
        biases.append(jnp.concatenate(halves, axis=1))
    for c in range(n_chain):
        q2_ref[c, :, LANES:] = _dot_tn(biases[c], place).astype(BF16)
    row_in_tile = lax.broadcasted_iota(jnp.int32, (2 * tq, 1), 0) & (tq - 1)
    kofs = lax.broadcasted_iota(jnp.int32, (1, MOBA_BLOCK), 1)

    first_head = lane < HEAD_DIM

    acc_ref[...] = jnp.zeros_like(acc_ref)
    m_ref[...] = jnp.full_like(m_ref, NEG)

    block_row = lax.broadcasted_iota(jnp.int32, (LANES, MOBA_BLOCK), 0)

    def score(c, j):
        bb, tile = chain(c)
        kt = k_ref[bb, tile, pl.ds(pl.multiple_of(j * MOBA_BLOCK, MOBA_BLOCK), MOBA_BLOCK)]
        return _dot(q2_ref[c], jnp.concatenate([kt, (block_row == j).astype(BF16)], axis=0))

    lead = s_ref.shape[0]
    for c in range(lead):
        s_ref[c] = score(c, 0)

    def step(j, causal):
        start = pl.multiple_of(j * MOBA_BLOCK, MOBA_BLOCK)
        scores = [s_ref[c] for c in range(lead)]
        for c in range(n_chain):
            if c + lead < n_chain:
                scores.append(score(c + lead, j))
            elif not causal:
                s_ref[c + lead - n_chain] = score(c + lead - n_chain, j + 1)
            bb, tile = chain(c)
            vs = v_ref[bb, pl.ds(start, MOBA_BLOCK), tile]
            s = scores[c]
            if causal:
                s = jnp.where(kofs <= row_in_tile, s, NEG)
            m_prev = m_ref[c]
            m_new = jnp.maximum(m_prev, jnp.max(s, axis=1, keepdims=True))
            alpha = jnp.exp(m_prev - m_new)
            p = jnp.exp(s - jnp.concatenate([m_new] * (MOBA_BLOCK // LANES), axis=1)).astype(BF16)
            ones = jnp.ones_like(vs)
            pv = [_dot(p[:tq], jnp.where(first_head, vs, ones)), _dot(p[tq:], jnp.where(first_head, ones, vs))]
            acc_ref[c] = alpha * acc_ref[c] + jnp.concatenate(pv, axis=0)
            m_ref[c] = m_new

    def past_step(j, carry):
        step(j, False)
        return carry

    lax.fori_loop(0, i, past_step, 0)
    step(i, True)
    for c in range(n_chain):
        bb, tile = chain(c)
        o_ref[bb, :, tile] = _merge_heads_and_norm(acc_ref[c], go_ref[:, tile], tq, True).astype(o_ref.dtype)


def _moba_call(qa, kat, vab, kmean, g_out, nb, n_tile):
    b, s, wa = qa.shape
    tq = MOBA_BLOCK
    nblk = s // MOBA_BLOCK
    w = n_tile * LANES
    n_chain = nb * n_tile
    stats = pltpu.VMEM((n_chain, 2 * tq, LANES), F32)
    resident = pl.Buffered(1) if (b // nb) * (wa // w) == 1 else None
    return pl.pallas_call(
        functools.partial(_moba_kernel, tq=tq, nblk=nblk, n_chain=n_chain),
        out_shape=jax.ShapeDtypeStruct(qa.shape, BF16),
        grid=(b // nb, wa // w, s // tq),
        in_specs=[pl.BlockSpec((nb, tq, w), lambda bi, hp, i: (bi, i, hp)),
                  pl.BlockSpec((nb, w, s), lambda bi, hp, i: (bi, hp, 0), pipeline_mode=resident),
                  pl.BlockSpec((nb, s, w), lambda bi, hp, i: (bi, 0, hp), pipeline_mode=resident),
                  pl.BlockSpec((nb, nblk, w), lambda bi, hp, i: (bi, 0, hp)),
                  pl.BlockSpec((1, w), lambda bi, hp, i: (0, hp))],
        out_specs=pl.BlockSpec((nb, tq, w), lambda bi, hp, i: (bi, i, hp)),
        scratch_shapes=[pltpu.VMEM((n_chain, 2 * tq, 2 * LANES), BF16),
                        pltpu.VMEM((MOBA_SCORE_LEAD, 2 * tq, MOBA_BLOCK), F32), stats, stats],
        compiler_params=_params(3),
        name="moba",
    )(qa, kat, vab, kmean, g_out)


def _sb_kernel(q_ref, k_ref, v_ref, go_ref, o_ref, q2_ref, acc_ref, car_ref, *, tq, n_chain):
    i = pl.program_id(2)
    lane = lax.broadcasted_iota(jnp.int32, (1, LANES), 1)
    for c in range(n_chain):
        q = q_ref[:, c * LANES:(c + 1) * LANES]
        for hh in range(HEADS_PER_TILE):
            q2_ref[c, hh * tq:(hh + 1) * tq, :] = jnp.where((lane // HEAD_DIM) == hh, q, jnp.zeros_like(q))
    row_in_tile = lax.broadcasted_iota(jnp.int32, (2 * tq, 1), 0) & (tq - 1)
    kofs = lax.broadcasted_iota(jnp.int32, (1, tq), 1)
    later = _later_matrix(tq)

    def step(start, tk, first):
        vis = kofs < row_in_tile
        zs = [_dot(q2_ref[c], k_ref[c * LANES:(c + 1) * LANES, pl.ds(start, tk)]) for c in range(n_chain)]
        mids = []
        for c in range(n_chain):
            sp = _softplus(zs[c])
            spk = jnp.where(vis, sp, 0.0) if first else sp
            mids.append((zs[c] - sp, _dot(spk.astype(BF16), later[:tk, :tk]), jnp.sum(spk, axis=1, keepdims=True)))
        for c in range(n_chain):
            log_beta, after, drop = mids[c]
            w = jnp.exp(log_beta - after)
            if first:
                w = jnp.where(vis, w, 0.0)
            pv = _dot(w.astype(BF16), v_ref[pl.ds(start, tk), c * LANES:(c + 1) * LANES])
            if first:
                acc_ref[c] = pv
                car_ref[c] = jnp.broadcast_to(-drop, pv.shape)
            else:
                car = car_ref[c]
                acc_ref[c] += jnp.exp(car) * pv
                car_ref[c] = car - drop

    step(pl.multiple_of(i * tq, tq), tq, True)

    def cond(state):
        t, car_max = state
        return jnp.logical_and(t < i, car_max > UNDERFLOW_LOG)

    def body(state):
        t, _ = state
        step(pl.multiple_of((i - 1 - t) * tq, tq), tq, False)
        return t + 1, jnp.max(car_ref[...])

    lax.while_loop(cond, body, (jnp.int32(0), jnp.max(car_ref[...])))
    for c in range(n_chain):
        tile = slice(c * LANES, (c + 1) * LANES)
        o_ref[:, tile] = _merge_heads_and_norm(acc_ref[c], go_ref[:, tile], tq).astype(o_ref.dtype)


def _sb_call(qb, kbb, vbb, g_out, b, s, n_chain):
    tq = MOBA_BLOCK
    nq = s // tq
    w = n_chain * LANES
    n_groups = qb.shape[1] // w
    stats = pltpu.VMEM((n_chain, 2 * tq, LANES), F32)
    return pl.pallas_call(
        functools.partial(_sb_kernel, tq=tq, n_chain=n_chain),
        out_shape=jax.ShapeDtypeStruct(qb.shape, BF16),
        grid=(b, n_groups, nq),
        in_specs=[pl.BlockSpec((tq, w), lambda bi, hp, i: (bi * nq + i, hp)),
                  pl.BlockSpec((None, w, s), lambda bi, hp, i: (bi, hp, 0)),
                  pl.BlockSpec((s, w), lambda bi, hp, i: (bi, hp)),
                  pl.BlockSpec((1, w), lambda bi, hp, i: (0, hp))],
        out_specs=pl.BlockSpec((tq, w), lambda bi, hp, i: (bi * nq + i, hp)),
        scratch_shapes=[pltpu.VMEM((n_chain, 2 * tq, LANES), BF16), stats, stats],
        compiler_params=_params(3),
        name="sb",
    )(qb, kbb, vbb, g_out)


def _block_diag_queries(q, n_heads):
    t, w = q.shape
    head_of_lane = lax.broadcasted_iota(jnp.int32, (n_heads, w), 1) // HEAD_DIM
    hmask = head_of_lane == lax.broadcasted_iota(jnp.int32, (n_heads, w), 0)
    rows = [jnp.where(hmask, jnp.broadcast_to(q[ti:ti + 1, :], (n_heads, w)), 0.0) for ti in range(t)]
    return jnp.concatenate(rows, axis=0), jnp.concatenate([hmask] * t, axis=0)


def _collapse_heads(o, hmask_rows, gain, n_tok, n_heads):
    oh = jnp.where(hmask_rows, o, 0.0)
    ss = jnp.sum(oh * oh, axis=1, keepdims=True) * (1.0 / HEAD_DIM)
    of = oh * lax.rsqrt(ss + LN_EPS)
    rows = [jnp.sum(of[ti * n_heads:(ti + 1) * n_heads, :], axis=0, keepdims=True) for ti in range(n_tok)]
    return jnp.concatenate(rows, axis=0) * gain


def _smoba_kernel(pt_ref, q_ref, kn_ref, vn_ref, go_ref, *rest, n_pages, n_tok, n_heads, group):
    del pt_ref
    o_ref = rest[2 * group * n_pages]
    for g in range(group):
        k_pages = rest[g * n_pages:(g + 1) * n_pages]
        v_pages = rest[(group + g) * n_pages:(group + g + 1) * n_pages]
        o_ref[g] = _smoba_element(q_ref[g], kn_ref[g], vn_ref[g], go_ref[...], k_pages, v_pages,
                                  n_pages, n_tok, n_heads).astype(o_ref.dtype)


def _smoba_element(q, kn, vn, gain, k_pages, v_pages, n_pages, n_tok, n_heads):
    qbd, hmask_rows = _block_diag_queries(q.astype(F32), n_heads)
    qbd_bf = qbd.astype(BF16)
    tok_of_row = lax.broadcasted_iota(jnp.int32, (n_tok * n_heads, 1), 0) // n_heads
    pages_per_block = MOBA_BLOCK // PAGE_SIZE
    n_blocks = n_pages // pages_per_block
    s_pages = [_dot(qbd_bf, k_pages[p][...].astype(BF16)) for p in range(n_pages)]
    gates = [sum(jnp.sum(s_pages[n * pages_per_block + u], axis=1, keepdims=True) for u in range(pages_per_block))
             for n in range(n_blocks)]
    biases = []
    for n in range(n_blocks):
        rank = jnp.zeros_like(gates[n])
        for n2 in range(n_blocks):
            if n2 != n:
                beats = (gates[n2] > gates[n]) | ((gates[n2] == gates[n]) & (n2 < n))
                rank = rank + beats.astype(F32)
        biases.append(jnp.where(rank < min(MOBA_TOPK, n_blocks), 0.0, NEG))
    s_new = [jnp.where(t <= tok_of_row, jnp.sum(qbd * kn[t:t + 1, :], axis=1, keepdims=True), NEG)
             for t in range(n_tok)]
    s_sel = [s_pages[p] + biases[p // pages_per_block] for p in range(n_pages)]
    m = functools.reduce(jnp.maximum, s_new)
    for p in range(n_pages):
        m = jnp.maximum(m, jnp.max(s_sel[p], axis=1, keepdims=True))
    acc = jnp.zeros(qbd.shape, F32)
    l = jnp.zeros_like(m)
    for t in range(n_tok):
        pt = jnp.exp(s_new[t] - m)
        l = l + pt
        acc = acc + pt * vn[t:t + 1, :]
    for p in range(n_pages):
        pp = jnp.exp(s_sel[p] - m)
        l = l + jnp.sum(pp, axis=1, keepdims=True)
        acc = acc + _dot_nt(pp.astype(BF16), v_pages[p][...].astype(BF16))
    return _collapse_heads(acc / l, hmask_rows, gain, n_tok, n_heads)


def _sb_new_tokens(qbd, kn, vn, n_tok, n_heads):
    tok_of_row = lax.broadcasted_iota(jnp.int32, (n_tok * n_heads, 1), 0) // n_heads
    acc = jnp.zeros(qbd.shape, F32)
    carry = jnp.zeros((n_tok * n_heads, 1), F32)
    for t in reversed(range(n_tok)):
        z = jnp.sum(qbd * kn[t:t + 1, :], axis=1, keepdims=True)
        sp = _softplus(z)
        vis = t < tok_of_row
        acc = acc + jnp.where(vis, jnp.exp(z - sp + carry), 0.0) * vn[t:t + 1, :]
        carry = carry - jnp.where(vis, sp, 0.0)
    return acc, carry


def _ssb_recent_kernel(pt_ref, q_ref, kn_ref, vn_ref, go_ref, *rest, n_recent, group, n_tok, n_heads):
    del pt_ref
    n_in = group * n_recent
    k_pages, v_pages = rest[:n_in], rest[n_in:2 * n_in]
    o_ref, car_ref = rest[2 * n_in:]
    later = _later_matrix(n_recent * PAGE_SIZE)
    gather = lambda pages, g: jnp.concatenate([pages[g * n_recent + u][...] for u in range(n_recent)],
                                              axis=1).astype(BF16)
    state = []
    for g in range(group):
        qbd, hmask_rows = _block_diag_queries(q_ref[g].astype(F32), n_heads)
        acc, carry = _sb_new_tokens(qbd, kn_ref[g], vn_ref[g], n_tok, n_heads)
        state.append((hmask_rows, acc, carry, _dot(qbd.astype(BF16), gather(k_pages, g))))
    mids = []
    for hmask_rows, acc, carry, z in state:
        sp = _softplus(z)
        mids.append((z - sp, _dot(sp.astype(BF16), later), jnp.sum(sp, axis=1, keepdims=True)))
    for g in range(group):
        hmask_rows, acc, carry, _ = state[g]
        log_beta, after, drop = mids[g]
        w = jnp.exp(log_beta - after)
        acc = acc + jnp.exp(carry) * _dot_nt(w.astype(BF16), gather(v_pages, g))
        o_ref[g] = _collapse_heads(acc, hmask_rows, go_ref[...], n_tok, n_heads).astype(o_ref.dtype)
        car_ref[g] = jnp.broadcast_to(jnp.max(carry - drop, axis=0, keepdims=True), car_ref.shape[1:])


def _ssb_recent_call(page_table, q, k_new, v_new, g_out, cache_k, cache_v, n_recent, group):
    db, n_pages = page_table.shape
    _, t, w = q.shape
    tok = lambda: pl.BlockSpec((group, t, w), lambda bi, pt: (bi, 0, 0))

    def page(g, u):
        return pl.BlockSpec((None, w, PAGE_SIZE),
                            lambda bi, pt: (pt[bi * group + g, n_pages - n_recent + u], 0, 0))

    pages = [page(g, u) for g in range(group) for u in range(n_recent)]
    grid_spec = pltpu.PrefetchScalarGridSpec(
        num_scalar_prefetch=1,
        grid=(db // group,),
        in_specs=[tok(), tok(), tok(), pl.BlockSpec((1, w), lambda bi, pt: (0, 0))] + pages * 2,
        out_specs=[tok(), pl.BlockSpec((group, 1, LANES), lambda bi, pt: (bi, 0, 0))],
    )
    return pl.pallas_call(
        functools.partial(_ssb_recent_kernel, n_recent=n_recent, group=group, n_tok=t, n_heads=w // HEAD_DIM),
        out_shape=[jax.ShapeDtypeStruct(q.shape, BF16), jax.ShapeDtypeStruct((db, 1, LANES), F32)],
        grid_spec=grid_spec,
        compiler_params=_params(1),
        name="ssb_recent",
    )(page_table, q, k_new, v_new, g_out, *([cache_k] * len(pages)), *([cache_v] * len(pages)))


def _ssb_kernel(pt_ref, q_ref, kn_ref, vn_ref, go_ref, *rest, n_pages, n_tok, n_heads, group):
    assert group == 1
    del pt_ref
    k_pages, v_pages = rest[:n_pages], rest[n_pages:2 * n_pages]
    o_ref, acc_ref, car_ref = rest[2 * n_pages:]
    qbd, hmask_rows = _block_diag_queries(q_ref[0].astype(F32), n_heads)
    qbd_bf = qbd.astype(BF16)
    later = _later_matrix(PAGE_SIZE)
    acc, carry = _sb_new_tokens(qbd, kn_ref[0], vn_ref[0], n_tok, n_heads)
    acc_ref[...] = acc
    car_ref[...] = jnp.broadcast_to(carry, car_ref.shape)
    for p in reversed(range(n_pages)):
        @pl.when(jnp.max(car_ref[...]) > UNDERFLOW_LOG)
        def _():
            z = _dot(qbd_bf, k_pages[p][...].astype(BF16))
            sp = _softplus(z)
            w = jnp.exp(z - sp - _dot(sp.astype(BF16), later))
            car = car_ref[...]
            acc_ref[...] += jnp.exp(car[:, :1]) * _dot_nt(w.astype(BF16), v_pages[p][...].astype(BF16))
            car_ref[...] = car - jnp.sum(sp, axis=1, keepdims=True)
    o_ref[0] = _collapse_heads(acc_ref[...], hmask_rows, go_ref[...], n_tok, n_heads).astype(o_ref.dtype)


def _sample_attn_call(body, name, page_table, q, k_new, v_new, g_out, cache_k, cache_v, group, scratch_shapes=()):
    db, n_pages = page_table.shape
    _, t, w = q.shape
    n_heads = w // HEAD_DIM
    tok = lambda: pl.BlockSpec((group, t, w), lambda bi, pt: (bi, 0, 0))

    def page(g, p):
        return pl.BlockSpec((None, w, PAGE_SIZE), lambda bi, pt: (pt[bi * group + g, p], 0, 0))

    pages = [page(g, p) for g in range(group) for p in range(n_pages)]
    grid_spec = pltpu.PrefetchScalarGridSpec(
        num_scalar_prefetch=1,
        grid=(db // group,),
        in_specs=[tok(), tok(), tok(), pl.BlockSpec((1, w), lambda bi, pt: (0, 0))] + pages * 2,
        out_specs=tok(),
        scratch_shapes=scratch_shapes,
    )
    return pl.pallas_call(
        functools.partial(body, n_pages=n_pages, n_tok=t, n_heads=n_heads, group=group),
        out_shape=jax.ShapeDtypeStruct(q.shape, BF16),
        grid_spec=grid_spec,
        compiler_params=_params(1),
        name=name,
    )(page_table, q, k_new, v_new, g_out, *([cache_k] * len(pages)), *([cache_v] * len(pages)))


def _post_kernel(x_ref, g0_ref, b0_ref, ma_ref, mb_ref, mod_ref, wo_ref, g1_ref, b1_ref,
                 wg_ref, wu_ref, wd_ref, g2_ref, b2_ref, o_ref, act_ref, *, alpha, wa, ff_chunk):
    xn = _layer_norm(x_ref[...], g0_ref[...], b0_ref[...])
    attn = _dot(ma_ref[...], wo_ref[:wa, :]) + _dot(mb_ref[...], wo_ref[wa:, :])
    x1 = _layer_norm(alpha * xn + mod_ref[2] * attn, g1_ref[...], b1_ref[...])
    hb = (x1 * (1.0 + mod_ref[4]) + mod_ref[3]).astype(BF16)
    for c in range(0, wg_ref.shape[1], ff_chunk):
        gate = _dot(hb, wg_ref[:, c:c + ff_chunk])
        up = _dot(hb, wu_ref[:, c:c + ff_chunk])
        act_ref[:, c:c + ff_chunk] = (gate / (1.0 + jnp.exp(-gate)) * up).astype(BF16)
    f = _dot(act_ref[...], wd_ref[...])
    o_ref[...] = _layer_norm(alpha * x1 + mod_ref[5] * f, g2_ref[...], b2_ref[...])


def _post_call(x2, g0, b0, ma, mb, mods, mod_spec, wo, g1, b1, wg, wu, wd, g2, b2, alpha, tm, ff_chunk):
    r, d = x2.shape
    wa = ma.shape[1]
    row = lambda w: pl.BlockSpec((tm, w), lambda i: (i, 0))
    vec = lambda: _const_spec((1, d))
    return pl.pallas_call(
        functools.partial(_post_kernel, alpha=alpha, wa=wa, ff_chunk=ff_chunk),
        out_shape=jax.ShapeDtypeStruct((r, d), F32),
        grid=(r // tm,),
        in_specs=[row(d), vec(), vec(), row(wa), row(mb.shape[1]), mod_spec, _const_spec(wo.shape), vec(), vec(),
                  _const_spec(wg.shape), _const_spec(wu.shape), _const_spec(wd.shape), vec(), vec()],
        out_specs=row(d),
        scratch_shapes=[pltpu.VMEM((tm, wg.shape[1]), BF16)],
        compiler_params=_params(1),
        name="post",
    )(x2, g0, b0, ma, mb, mods, wo, g1, b1, wg, wu, wd, g2, b2)


def _rotary_tables(pos):
    half = ROT_DIM // 2
    inv = ROPE_THETA ** (-jnp.arange(half, dtype=F32) * 2.0 / ROT_DIM)
    ang = pos.astype(F32)[:, None] * inv
    cos, sin = jnp.cos(ang), jnp.sin(ang)
    n = pos.shape[0]
    rest = HEAD_DIM - ROT_DIM
    c = jnp.concatenate([cos, cos, jnp.ones((n, rest), F32)], axis=1)
    s_up = jnp.concatenate([-sin, jnp.zeros((n, half + rest), F32)], axis=1)
    s_dn = jnp.concatenate([jnp.zeros((n, half), F32), sin, jnp.zeros((n, rest), F32)], axis=1)
    return jnp.stack([jnp.tile(t, (1, HEADS_PER_TILE)) for t in (c, s_up, s_dn)])


def kernel(x_prompt, x_sample, cache_moba_k, cache_moba_v, cache_sb_k, cache_sb_v, page_table, c_prompt, c_sample,
           ln0_g, ln0_b, w_mod, b_mod, w_in, g_out_a, g_out_b, w_out, ln1_g, ln1_b, w_gate, w_up, w_down,
           ln2_g, ln2_b):
    b, s, d = x_prompt.shape
    db, t, _ = x_sample.shape
    depth = w_mod.shape[0]
    assert depth == 1, "ln0 is fused into the first layer's kernels; only a one-layer trunk is supported"
    wa, wb = g_out_a.shape[1], g_out_b.shape[1]
    ha, hb = wa // HEAD_DIM, wb // HEAD_DIM
    n_pages = page_table.shape[1]
    past = n_pages * PAGE_SIZE
    assert s % MOBA_BLOCK == 0 and past % MOBA_BLOCK == 0
    alpha = (2 * depth) ** 0.25
    vec = lambda a: a.reshape(1, -1)

    n_mod = 6
    rows = b + db
    pad = -rows % 8
    c_all = jnp.concatenate([c_prompt, c_sample, jnp.zeros((pad, d), F32)], axis=0)
    m = _mod_call(c_all, w_mod[0], b_mod[0].reshape(1, -1))
    mods_p = m[:b].reshape(b, n_mod, 1, d)
    mods_s = jnp.repeat(m[b:rows].reshape(db, n_mod, d), t, axis=0).transpose(1, 0, 2)

    w_in_bf = w_in[0].astype(BF16)
    wo_bf, wg_bf, wu_bf, wd_bf = (w[0].astype(BF16) for w in (w_out, w_gate, w_up, w_down))
    g0, b0 = vec(ln0_g), vec(ln0_b)
    ln = [vec(a[0]) for a in (ln1_g, ln1_b, ln2_g, ln2_b)]
    goa, gob = g_out_a[0].reshape(1, wa), g_out_b[0].reshape(1, wb)

    tm = 512
    ff_chunk = 256

    xp2 = x_prompt.reshape(b * s, d)
    mod_spec_p = pl.BlockSpec((None, n_mod, 1, d), lambda i: (i // (s // tm), 0, 0, 0))
    rot_p = _rotary_tables(jnp.arange(s, dtype=jnp.int32))
    (qa, qb, ka, va, kb, vb, kab, vab, kbb, vbb, kmean) = _pre_call(
        xp2, g0, b0, mods_p, mod_spec_p, rot_p, w_in_bf, wa, wb, tm, True)
    mixed_a = _moba_call(qa.reshape(b, s, wa), kab, vab.reshape(b, s, wa), kmean.reshape(b, s // MOBA_BLOCK, wa),
                         goa, nb=1, n_tile=wa // LANES).reshape(b * s, wa)
    mixed_b = _sb_call(qb, kbb, vbb, gob, b, s, n_chain=4)
    y_p = _post_call(xp2, g0, b0, mixed_a, mixed_b, mods_p, mod_spec_p, wo_bf, ln[0], ln[1],
                     wg_bf, wu_bf, wd_bf, ln[2], ln[3], alpha, tm, ff_chunk)

    rs = db * t
    xs2 = x_sample.reshape(rs, d)
    mod_spec_s = pl.BlockSpec((n_mod, rs, d), lambda i: (0, 0, 0))
    rot_s = _rotary_tables(jnp.tile(past + jnp.arange(t, dtype=jnp.int32), db))
    (sqa, sqb, ska, sva, skb, svb) = _pre_call(
        xs2, g0, b0, mods_s, mod_spec_s, rot_s, w_in_bf, wa, wb, rs, False)
    tok3 = lambda a: a.reshape(db, t, a.shape[-1])
    pages = lambda c: c[0].transpose(0, 2, 3, 1).reshape(c.shape[1], -1, PAGE_SIZE)
    smixed_a = _sample_attn_call(_smoba_kernel, "smoba", page_table, tok3(sqa), tok3(ska), tok3(sva), goa,
                                 pages(cache_moba_k), pages(cache_moba_v), group=SMOBA_GROUP)
    sb_args = (page_table, tok3(sqb), tok3(skb), tok3(svb), gob, pages(cache_sb_k), pages(cache_sb_v))
    recent_b, log_keep = _ssb_recent_call(*sb_args, n_recent=min(SB_RECENT_PAGES, n_pages), group=SB_GROUP)
    smixed_b = lax.cond(
        jnp.max(log_keep) > UNDERFLOW_LOG,
        lambda: _sample_attn_call(_ssb_kernel, "ssb", *sb_args, group=1, scratch_shapes=[
            pltpu.VMEM((t * hb, wb), F32), pltpu.VMEM((t * hb, LANES), F32)]),
        lambda: recent_b)
    y_s = _post_call(xs2, g0, b0, smixed_a.reshape(rs, wa), smixed_b.reshape(rs, wb), mods_s, mod_spec_s,
                     wo_bf, ln[0], ln[1], wg_bf, wu_bf, wd_bf, ln[2], ln[3], alpha, rs, ff_chunk)

    kv_p = lambda a, h: a.reshape(b, h, HEAD_DIM, s).transpose(0, 3, 1, 2)[None]
    return (y_p.reshape(b, s, d), y_s.reshape(db, t, d),
            kv_p(ka, ha), kv_p(va, ha), kv_p(kb, hb), kv_p(vb, hb),
            ska.reshape(1, db, t, ha, HEAD_DIM), sva.reshape(1, db, t, ha, HEAD_DIM),
            skb.reshape(1, db, t, hb, HEAD_DIM), svb.reshape(1, db, t, hb, HEAD_DIM))
```

```python
import functools

import jax
import jax.numpy as jnp
from jax import lax
from jax.experimental import pallas as pl
from jax.experimental.pallas import tpu as pltpu

F32 = jnp.float32
BF16 = jnp.bfloat16

HEAD_DIM = 64
MOBA_BLOCK = 256
MOBA_TOPK = 3
PAGE_SIZE = 128
ROPE_THETA = 500000.0
ROT_DIM = HEAD_DIM // 4
LN_EPS = 1e-5
LANES = 128
HEADS_PER_TILE = LANES // HEAD_DIM
NEG = -1e30
UNDERFLOW_LOG = -110.0
MOBA_SCORE_LEAD = 4
SMOBA_GROUP = 2
SB_RECENT_PAGES = 2
SB_GROUP = 8
VMEM_LIMIT = 56 * 1024 * 1024


def _dot(a, b):
    return jnp.dot(a, b, preferred_element_type=F32)


def _dot_nt(a, b):
    return lax.dot_general(a, b, (((1,), (1,)), ((), ())), preferred_element_type=F32)


def _dot_tn(a, b):
    return lax.dot_general(a, b, (((0,), (0,)), ((), ())), preferred_element_type=F32)


def _split_bf16(x):
    hi = x.astype(BF16)
    lo = (x - hi.astype(F32)).astype(BF16)
    return hi, lo


def _layer_norm(x, g, b):
    mu = jnp.mean(x, axis=-1, keepdims=True)
    xc = x - mu
    var = jnp.mean(xc * xc, axis=-1, keepdims=True)
    return xc * lax.rsqrt(var + LN_EPS) * g + b


def _softplus(z):
    return jnp.maximum(z, 0.0) + jnp.log(1.0 + jnp.exp(-jnp.abs(z)))


def _later_matrix(n):
    return (lax.broadcasted_iota(jnp.int32, (n, n), 0) > lax.broadcasted_iota(jnp.int32, (n, n), 1)).astype(BF16)


def _params(n_grid):
    return pltpu.CompilerParams(dimension_semantics=("arbitrary",) * n_grid, vmem_limit_bytes=VMEM_LIMIT)


def _const_spec(shape):
    zeros = (0,) * len(shape)
    return pl.BlockSpec(shape, lambda *_: zeros, pipeline_mode=pl.Buffered(1))


def _mod_kernel(c_ref, w_ref, b_ref, o_ref):
    c = c_ref[...]
    a = c / (1.0 + jnp.exp(-c))
    o_ref[...] = _dot(a.astype(BF16), w_ref[...].astype(BF16)) + b_ref[...]


def _mod_call(c_all, w_mod, b_mod, tn=1536):
    r, d = c_all.shape
    n = w_mod.shape[1]
    return pl.pallas_call(
        _mod_kernel,
        out_shape=jax.ShapeDtypeStruct((r, n), F32),
        grid=(n // tn,),
        in_specs=[pl.BlockSpec((r, d), lambda j: (0, 0)),
                  pl.BlockSpec((d, tn), lambda j: (0, j)),
                  pl.BlockSpec((1, tn), lambda j: (0, j))],
        out_specs=pl.BlockSpec((r, tn), lambda j: (0, j)),
        compiler_params=_params(1),
        name="mod",
    )(c_all, w_mod, b_mod)


def _pre_kernel(x_ref, g0_ref, b0_ref, mod_ref, rot_ref, w_ref, qa_ref, qb_ref, ka_ref, va_ref, kb_ref, vb_ref,
                *extra_refs, wa, wb):
    transposed = bool(extra_refs)

    def put(ref, sl, val):
        if transposed:
            ref[sl, :] = val.T
        else:
            ref[:, sl] = val

    xn = _layer_norm(x_ref[...], g0_ref[...], b0_ref[...])
    h = xn * (1.0 + mod_ref[1]) + mod_ref[0]
    qkv = _dot(h.astype(BF16), w_ref[...])
    cos, s_up, s_dn = rot_ref[0], rot_ref[1], rot_ref[2]

    def rotate(t):
        return t * cos + pltpu.roll(t, LANES - ROT_DIM // 2, 1) * s_up + pltpu.roll(t, ROT_DIM // 2, 1) * s_dn

    for t in range(wa // LANES):
        sl = slice(t * LANES, (t + 1) * LANES)
        qa = rotate(qkv[:, sl])
        ka = rotate(qkv[:, wa + t * LANES:wa + (t + 1) * LANES])
        qa_ref[:, sl] = (qa * HEAD_DIM ** -0.5).astype(BF16)
        put(ka_ref, sl, ka)
        if extra_refs:
            kab_ref, km_ref = extra_refs[0], extra_refs[4]
            kab_ref[sl, :] = ka.T.astype(BF16)
            for g in range(km_ref.shape[0]):
                blk = ka[g * MOBA_BLOCK:(g + 1) * MOBA_BLOCK, :]
                km_ref[g, :, sl] = jnp.sum(blk, axis=0, keepdims=True) * (1.0 / MOBA_BLOCK)
    o = 3 * wa
    va, kb, vb = qkv[:, 2 * wa:o], qkv[:, o + wb:o + 2 * wb], qkv[:, o + 2 * wb:o + 3 * wb]
    qb_ref[...] = (qkv[:, o:o + wb] * HEAD_DIM ** -0.5).astype(BF16)
    for ref, val in ((va_ref, va), (kb_ref, kb), (vb_ref, vb)):
        for t in range(val.shape[1] // LANES):
            sl = slice(t * LANES, (t + 1) * LANES)
            put(ref, sl, val[:, sl])
    if extra_refs:
        extra_refs[1][...] = va.astype(BF16)
        for t in range(wb // LANES):
            sl = slice(t * LANES, (t + 1) * LANES)
            extra_refs[2][sl, :] = kb[:, sl].T.astype(BF16)
        extra_refs[3][...] = vb.astype(BF16)


def _pre_call(x2, g0, b0, mods, mod_spec, rot, w_in_bf, wa, wb, tm, prompt):
    r, d = x2.shape
    n_rot = rot.shape[1] // tm
    row = lambda w: pl.BlockSpec((tm, w), lambda i: (i, 0))
    if prompt:
        nt = rot.shape[1] // tm
        kv = lambda w: jax.ShapeDtypeStruct((r // rot.shape[1], w, rot.shape[1]), F32)
        kv_spec = lambda w: pl.BlockSpec((None, w, tm), lambda i: (i // nt, 0, i % nt))
    else:
        kv = lambda w: jax.ShapeDtypeStruct((r, w), F32)
        kv_spec = row
    outs = [jax.ShapeDtypeStruct((r, wa), BF16), jax.ShapeDtypeStruct((r, wb), BF16), kv(wa), kv(wa), kv(wb), kv(wb)]
    out_specs = [row(wa), row(wb), kv_spec(wa), kv_spec(wa), kv_spec(wb), kv_spec(wb)]
    if prompt:
        kt = lambda w: jax.ShapeDtypeStruct((r // rot.shape[1], w, rot.shape[1]), BF16)
        outs += [kt(wa), jax.ShapeDtypeStruct((r, wa), BF16), kt(wb), jax.ShapeDtypeStruct((r, wb), BF16),
                 jax.ShapeDtypeStruct((r // MOBA_BLOCK, 1, wa), F32)]
        out_specs += [kv_spec(wa), row(wa), kv_spec(wb), row(wb),
                      pl.BlockSpec((tm // MOBA_BLOCK, 1, wa), lambda i: (i, 0, 0))]
    return pl.pallas_call(
        functools.partial(_pre_kernel, wa=wa, wb=wb),
        out_shape=outs,
        grid=(r // tm,),
        in_specs=[row(d), _const_spec((1, d)), _const_spec((1, d)), mod_spec,
                  pl.BlockSpec((3, tm, LANES), lambda i: (0, i % n_rot, 0)),
                  _const_spec(w_in_bf.shape)],
        out_specs=out_specs,
        compiler_params=_params(1),
        name="pre",
    )(x2, g0, b0, mods, rot, w_in_bf)


def _merge_heads_and_norm(o2, gain, tq, denominators_in_other_lanes=False):
    lane = lax.broadcasted_iota(jnp.int32, (1, LANES), 1)
    first = lane < HEAD_DIM
    top, bot = o2[:tq], o2[tq:]
    o = jnp.where(first, top, bot)
    if denominators_in_other_lanes:
        o = o / jnp.where(first, pltpu.roll(top, HEAD_DIM, 1), pltpu.roll(bot, HEAD_DIM, 1))
    sq = o * o
    ss0 = jnp.sum(jnp.where(first, sq, 0.0), axis=1, keepdims=True)
    ss1 = jnp.sum(jnp.where(first, 0.0, sq), axis=1, keepdims=True)
    scale = jnp.where(first, lax.rsqrt(ss0 * (1.0 / HEAD_DIM) + LN_EPS), lax.rsqrt(ss1 * (1.0 / HEAD_DIM) + LN_EPS))
    return o * scale * gain


def _moba_kernel(q_ref, k_ref, v_ref, km_ref, go_ref, o_ref, q2_ref, s_ref, acc_ref, m_ref, *,
                 tq, nblk, n_chain):
    i = pl.program_id(2)
    n_tile = q_ref.shape[2] // LANES

    def chain(c):
        t = c % n_tile
        return c // n_tile, slice(t * LANES, (t + 1) * LANES)

    i_f = i.astype(F32)
    lane = lax.broadcasted_iota(jnp.int32, (1, LANES), 1)
    n_idx = lax.broadcasted_iota(jnp.int32, (nblk, tq), 0).astype(F32)
    valid = n_idx < i_f
    place = (lax.broadcasted_iota(jnp.int32, (nblk, LANES), 1)
             == lax.broadcasted_iota(jnp.int32, (nblk, LANES), 0)).astype(BF16)
    gates = []
    for c in range(n_chain):
        bb, tile = chain(c)
        q = q_ref[bb, :, tile]
        for hh in range(HEADS_PER_TILE):
            q2_ref[c, hh * tq:(hh + 1) * tq, :LANES] = jnp.where((lane // HEAD_DIM) == hh, q, jnp.zeros_like(q))
        g2 = _dot_nt(jnp.concatenate(_split_bf16(km_ref[bb, :, tile]), axis=0), q2_ref[c, :, :LANES])
        gates.append(g2[:nblk] + g2[nblk:])
    biases = []
    for c in range(n_chain):
        halves = []
        for hh in range(HEADS_PER_TILE):
            g = gates[c][:, hh * tq:(hh + 1) * tq]
            cand = valid
            sel = n_idx == i_f
            for _ in range(MOBA_TOPK):
                gm = jnp.where(cand, g, -jnp.inf)
                best = jnp.max(gm, axis=0, keepdims=True)
                first = jnp.min(jnp.where(cand & (gm == best), n_idx, float(nblk)), axis=0, keepdims=True)
                pick = n_idx == first
                sel = sel | pick
                cand = cand & jnp.logical_not(pick)
            halves.append(jnp.where(sel, 0.0, NEG).astype(BF16))
        biases.append(jnp.concatenate(halves, axis=1))
    for c in range(n_chain):
        q2_ref[c, :, LANES:] = _dot_tn(biases[c], place).astype(BF16)
    row_in_tile = lax.broadcasted_iota(jnp.int32, (2 * tq, 1), 0) & (tq - 1)
    kofs = lax.broadcasted_iota(jnp.int32, (1, MOBA_BLOCK), 1)

    first_head = lane < HEAD_DIM

    acc_ref[...] = jnp.zeros_like(acc_ref)
    m_ref[...] = jnp.full_like(m_ref, NEG)

    block_row = lax.broadcasted_iota(jnp.int32, (LANES, MOBA_BLOCK), 0)

    def score(c, j):
        bb, tile = chain(c)
        kt = k_ref[bb, tile, pl.ds(pl.multiple_of(j * MOBA_BLOCK, MOBA_BLOCK), MOBA_BLOCK)]
        return _dot(q2_ref[c], jnp.concatenate([kt, (block_row == j).astype(BF16)], axis=0))

    lead = s_ref.shape[0]
    for c in range(lead):
        s_ref[c] = score(c, 0)

    def step(j, causal):
        start = pl.multiple_of(j * MOBA_BLOCK, MOBA_BLOCK)
        scores = [s_ref[c] for c in range(lead)]
        for c in range(n_chain):
            if c + lead < n_chain:
                scores.append(score(c + lead, j))
            elif not causal:
                s_ref[c + lead - n_chain] = score(c + lead - n_chain, j + 1)
            bb, tile = chain(c)
            vs = v_ref[bb, pl.ds(start, MOBA_BLOCK), tile]
            s = scores[c]
            if causal:
                s = jnp.where(kofs <= row_in_tile, s, NEG)
            m_prev = m_ref[c]
            m_new = jnp.maximum(m_prev, jnp.max(s, axis=1, keepdims=True))
            alpha = jnp.exp(m_prev - m_new)
            p = jnp.exp(s - jnp.concatenate([m_new] * (MOBA_BLOCK // LANES), axis=1)).astype(BF16)
            ones = jnp.ones_like(vs)
            pv = [_dot(p[:tq], jnp.where(first_head, vs, ones)), _dot(p[tq:], jnp.where(first_head, ones, vs))]
            acc_ref[c] = alpha * acc_ref[c] + jnp.concatenate(pv, axis=0)
            m_ref[c] = m_new

    def past_step(j, carry):
        step(j, False)
        return carry

    lax.fori_loop(0, i, past_step, 0)
    step(i, True)
    for c in range(n_chain):
        bb, tile = chain(c)
        o_ref[bb, :, tile] = _merge_heads_and_norm(acc_ref[c], go_ref[:, tile], tq, True).astype(o_ref.dtype)


def _moba_call(qa, kat, vab, kmean, g_out, nb, n_tile):
    b, s, wa = qa.shape
    tq = MOBA_BLOCK
    nblk = s // MOBA_BLOCK
    w = n_tile * LANES
    n_chain = nb * n_tile
    stats = pltpu.VMEM((n_chain, 2 * tq, LANES), F32)
    resident = pl.Buffered(1) if (b // nb) * (wa // w) == 1 else None
    return pl.pallas_call(
        functools.partial(_moba_kernel, tq=tq, nblk=nblk, n_chain=n_chain),
        out_shape=jax.ShapeDtypeStruct(qa.shape, BF16),
        grid=(b // nb, wa // w, s // tq),
        in_specs=[pl.BlockSpec((nb, tq, w), lambda bi, hp, i: (bi, i, hp)),
                  pl.BlockSpec((nb, w, s), lambda bi, hp, i: (bi, hp, 0), pipeline_mode=resident),
                  pl.BlockSpec((nb, s, w), lambda bi, hp, i: (bi, 0, hp), pipeline_mode=resident),
                  pl.BlockSpec((nb, nblk, w), lambda bi, hp, i: (bi, 0, hp)),
                  pl.BlockSpec((1, w), lambda bi, hp, i: (0, hp))],
        out_specs=pl.BlockSpec((nb, tq, w), lambda bi, hp, i: (bi, i, hp)),
        scratch_shapes=[pltpu.VMEM((n_chain, 2 * tq, 2 * LANES), BF16),
                        pltpu.VMEM((MOBA_SCORE_LEAD, 2 * tq, MOBA_BLOCK), F32), stats, stats],
        compiler_params=_params(3),
        name="moba",
    )(qa, kat, vab, kmean, g_out)


def _sb_kernel(q_ref, k_ref, v_ref, go_ref, o_ref, q2_ref, acc_ref, car_ref, *, tq, n_chain):
    i = pl.program_id(2)
    lane = lax.broadcasted_iota(jnp.int32, (1, LANES), 1)
    for c in range(n_chain):
        q = q_ref[:, c * LANES:(c + 1) * LANES]
        for hh in range(HEADS_PER_TILE):
            q2_ref[c, hh * tq:(hh + 1) * tq, :] = jnp.where((lane // HEAD_DIM) == hh, q, jnp.zeros_like(q))
    row_in_tile = lax.broadcasted_iota(jnp.int32, (2 * tq, 1), 0) & (tq - 1)
    kofs = lax.broadcasted_iota(jnp.int32, (1, tq), 1)
    later = _later_matrix(tq)

    def step(start, tk, first):
        vis = kofs < row_in_tile

        def scores(c):
            return _dot(q2_ref[c], k_ref[c * LANES:(c + 1) * LANES, pl.ds(start, tk)])

        def middle(z):
            sp = _softplus(z)
            spk = jnp.where(vis, sp, 0.0) if first else sp
            return z - sp, _dot(spk.astype(BF16), later[:tk, :tk]), jnp.sum(spk, axis=1, keepdims=True)

        zs, mids = [scores(0)], []
        for k in range(1, n_chain + 2):
            if k < n_chain:
                zs.append(scores(k))
            if k - 1 < n_chain:
                mids.append(middle(zs[k - 1]))
            if k < 2:
                continue
            c = k - 2
            log_beta, after, drop = mids[c]
            w = jnp.exp(log_beta - after)
            if first:
                w = jnp.where(vis, w, 0.0)
            pv = _dot(w.astype(BF16), v_ref[pl.ds(start, tk), c * LANES:(c + 1) * LANES])
            if first:
                acc_ref[c] = pv
                car_ref[c] = jnp.broadcast_to(-drop, pv.shape)
            else:
                car = car_ref[c]
                acc_ref[c] += jnp.exp(car) * pv
                car_ref[c] = car - drop

    step(pl.multiple_of(i * tq, tq), tq, True)

    def cond(state):
        t, car_max = state
        return jnp.logical_and(t < i, car_max > UNDERFLOW_LOG)

    def body(state):
        t, _ = state
        step(pl.multiple_of((i - 1 - t) * tq, tq), tq, False)
        return t + 1, jnp.max(car_ref[...])

    lax.while_loop(cond, body, (jnp.int32(0), jnp.max(car_ref[...])))
    for c in range(n_chain):
        tile = slice(c * LANES, (c + 1) * LANES)
        o_ref[:, tile] = _merge_heads_and_norm(acc_ref[c], go_ref[:, tile], tq).astype(o_ref.dtype)


def _sb_call(qb, kbb, vbb, g_out, b, s, n_chain):
    tq = MOBA_BLOCK
    nq = s // tq
    w = n_chain * LANES
    n_groups = qb.shape[1] // w
    stats = pltpu.VMEM((n_chain, 2 * tq, LANES), F32)
    return pl.pallas_call(
        functools.partial(_sb_kernel, tq=tq, n_chain=n_chain),
        out_shape=jax.ShapeDtypeStruct(qb.shape, BF16),
        grid=(b, n_groups, nq),
        in_specs=[pl.BlockSpec((tq, w), lambda bi, hp, i: (bi * nq + i, hp)),
                  pl.BlockSpec((None, w, s), lambda bi, hp, i: (bi, hp, 0)),
                  pl.BlockSpec((s, w), lambda bi, hp, i: (bi, hp)),
                  pl.BlockSpec((1, w), lambda bi, hp, i: (0, hp))],
        out_specs=pl.BlockSpec((tq, w), lambda bi, hp, i: (bi * nq + i, hp)),
        scratch_shapes=[pltpu.VMEM((n_chain, 2 * tq, LANES), BF16), stats, stats],
        compiler_params=_params(3),
        name="sb",
    )(qb, kbb, vbb, g_out)


def _block_diag_queries(q, n_heads):
    t, w = q.shape
    head_of_lane = lax.broadcasted_iota(jnp.int32, (n_heads, w), 1) // HEAD_DIM
    hmask = head_of_lane == lax.broadcasted_iota(jnp.int32, (n_heads, w), 0)
    rows = [jnp.where(hmask, jnp.broadcast_to(q[ti:ti + 1, :], (n_heads, w)), 0.0) for ti in range(t)]
    return jnp.concatenate(rows, axis=0), jnp.concatenate([hmask] * t, axis=0)


def _collapse_heads(o, hmask_rows, gain, n_tok, n_heads):
    oh = jnp.where(hmask_rows, o, 0.0)
    ss = jnp.sum(oh * oh, axis=1, keepdims=True) * (1.0 / HEAD_DIM)
    of = oh * lax.rsqrt(ss + LN_EPS)
    rows = [jnp.sum(of[ti * n_heads:(ti + 1) * n_heads, :], axis=0, keepdims=True) for ti in range(n_tok)]
    return jnp.concatenate(rows, axis=0) * gain


def _smoba_kernel(pt_ref, q_ref, kn_ref, vn_ref, go_ref, *rest, n_pages, n_tok, n_heads, group):
    del pt_ref
    o_ref = rest[2 * group * n_pages]
    for g in range(group):
        k_pages = rest[g * n_pages:(g + 1) * n_pages]
        v_pages = rest[(group + g) * n_pages:(group + g + 1) * n_pages]
        o_ref[g] = _smoba_element(q_ref[g], kn_ref[g], vn_ref[g], go_ref[...], k_pages, v_pages,
                                  n_pages, n_tok, n_heads).astype(o_ref.dtype)


def _smoba_element(q, kn, vn, gain, k_pages, v_pages, n_pages, n_tok, n_heads):
    qbd, hmask_rows = _block_diag_queries(q.astype(F32), n_heads)
    qbd_bf = qbd.astype(BF16)
    tok_of_row = lax.broadcasted_iota(jnp.int32, (n_tok * n_heads, 1), 0) // n_heads
    pages_per_block = MOBA_BLOCK // PAGE_SIZE
    n_blocks = n_pages // pages_per_block
    s_pages = [_dot(qbd_bf, k_pages[p][...].astype(BF16)) for p in range(n_pages)]
    gates = [sum(jnp.sum(s_pages[n * pages_per_block + u], axis=1, keepdims=True) for u in range(pages_per_block))
             for n in range(n_blocks)]
    biases = []
    for n in range(n_blocks):
        rank = jnp.zeros_like(gates[n])
        for n2 in range(n_blocks):
            if n2 != n:
                beats = (gates[n2] > gates[n]) | ((gates[n2] == gates[n]) & (n2 < n))
                rank = rank + beats.astype(F32)
        biases.append(jnp.where(rank < min(MOBA_TOPK, n_blocks), 0.0, NEG))
    s_new = [jnp.where(t <= tok_of_row, jnp.sum(qbd * kn[t:t + 1, :], axis=1, keepdims=True), NEG)
             for t in range(n_tok)]
    s_sel = [s_pages[p] + biases[p // pages_per_block] for p in range(n_pages)]
    m = functools.reduce(jnp.maximum, s_new)
    for p in range(n_pages):
        m = jnp.maximum(m, jnp.max(s_sel[p], axis=1, keepdims=True))
    acc = jnp.zeros(qbd.shape, F32)
    l = jnp.zeros_like(m)
    for t in range(n_tok):
        pt = jnp.exp(s_new[t] - m)
        l = l + pt
        acc = acc + pt * vn[t:t + 1, :]
    for p in range(n_pages):
        pp = jnp.exp(s_sel[p] - m)
        l = l + jnp.sum(pp, axis=1, keepdims=True)
        acc = acc + _dot_nt(pp.astype(BF16), v_pages[p][...].astype(BF16))
    return _collapse_heads(acc / l, hmask_rows, gain, n_tok, n_heads)


def _sb_new_tokens(qbd, kn, vn, n_tok, n_heads):
    tok_of_row = lax.broadcasted_iota(jnp.int32, (n_tok * n_heads, 1), 0) // n_heads
    acc = jnp.zeros(qbd.shape, F32)
    carry = jnp.zeros((n_tok * n_heads, 1), F32)
    for t in reversed(range(n_tok)):
        z = jnp.sum(qbd * kn[t:t + 1, :], axis=1, keepdims=True)
        sp = _softplus(z)
        vis = t < tok_of_row
        acc = acc + jnp.where(vis, jnp.exp(z - sp + carry), 0.0) * vn[t:t + 1, :]
        carry = carry - jnp.where(vis, sp, 0.0)
    return acc, carry


def _ssb_recent_kernel(pt_ref, q_ref, kn_ref, vn_ref, go_ref, *rest, n_recent, group, n_tok, n_heads):
    del pt_ref
    n_in = group * n_recent
    k_pages, v_pages = rest[:n_in], rest[n_in:2 * n_in]
    o_ref, car_ref = rest[2 * n_in:]
    later = _later_matrix(n_recent * PAGE_SIZE)
    gather = lambda pages, g: jnp.concatenate([pages[g * n_recent + u][...] for u in range(n_recent)],
                                              axis=1).astype(BF16)
    state = []
    for g in range(group):
        qbd, hmask_rows = _block_diag_queries(q_ref[g].astype(F32), n_heads)
        acc, carry = _sb_new_tokens(qbd, kn_ref[g], vn_ref[g], n_tok, n_heads)
        state.append((hmask_rows, acc, carry, _dot(qbd.astype(BF16), gather(k_pages, g))))
    mids = []
    for hmask_rows, acc, carry, z in state:
        sp = _softplus(z)
        mids.append((z - sp, _dot(sp.astype(BF16), later), jnp.sum(sp, axis=1, keepdims=True)))
    for g in range(group):
        hmask_rows, acc, carry, _ = state[g]
        log_beta, after, drop = mids[g]
        w = jnp.exp(log_beta - after)
        acc = acc + jnp.exp(carry) * _dot_nt(w.astype(BF16), gather(v_pages, g))
        o_ref[g] = _collapse_heads(acc, hmask_rows, go_ref[...], n_tok, n_heads).astype(o_ref.dtype)
        car_ref[g] = jnp.broadcast_to(jnp.max(carry - drop, axis=0, keepdims=True), car_ref.shape[1:])


def _ssb_recent_call(page_table, q, k_new, v_new, g_out, cache_k, cache_v, n_recent, group):
    db, n_pages = page_table.shape
    _, t, w = q.shape
    tok = lambda: pl.BlockSpec((group, t, w), lambda bi, pt: (bi, 0, 0))

    def page(g, u):
        return pl.BlockSpec((None, w, PAGE_SIZE),
                            lambda bi, pt: (pt[bi * group + g, n_pages - n_recent + u], 0, 0))

    pages = [page(g, u) for g in range(group) for u in range(n_recent)]
    grid_spec = pltpu.PrefetchScalarGridSpec(
        num_scalar_prefetch=1,
        grid=(db // group,),
        in_specs=[tok(), tok(), tok(), pl.BlockSpec((1, w), lambda bi, pt: (0, 0))] + pages * 2,
        out_specs=[tok(), pl.BlockSpec((group, 1, LANES), lambda bi, pt: (bi, 0, 0))],
    )
    return pl.pallas_call(
        functools.partial(_ssb_recent_kernel, n_recent=n_recent, group=group, n_tok=t, n_heads=w // HEAD_DIM),
        out_shape=[jax.ShapeDtypeStruct(q.shape, BF16), jax.ShapeDtypeStruct((db, 1, LANES), F32)],
        grid_spec=grid_spec,
        compiler_params=_params(1),
        name="ssb_recent",
    )(page_table, q, k_new, v_new, g_out, *([cache_k] * len(pages)), *([cache_v] * len(pages)))


def _ssb_kernel(pt_ref, q_ref, kn_ref, vn_ref, go_ref, *rest, n_pages, n_tok, n_heads, group):
    assert group == 1
    del pt_ref
    k_pages, v_pages = rest[:n_pages], rest[n_pages:2 * n_pages]
    o_ref, acc_ref, car_ref = rest[2 * n_pages:]
    qbd, hmask_rows = _block_diag_queries(q_ref[0].astype(F32), n_heads)
    qbd_bf = qbd.astype(BF16)
    later = _later_matrix(PAGE_SIZE)
    acc, carry = _sb_new_tokens(qbd, kn_ref[0], vn_ref[0], n_tok, n_heads)
    acc_ref[...] = acc
    car_ref[...] = jnp.broadcast_to(carry, car_ref.shape)
    for p in reversed(range(n_pages)):
        @pl.when(jnp.max(car_ref[...]) > UNDERFLOW_LOG)
        def _():
            z = _dot(qbd_bf, k_pages[p][...].astype(BF16))
            sp = _softplus(z)
            w = jnp.exp(z - sp - _dot(sp.astype(BF16), later))
            car = car_ref[...]
            acc_ref[...] += jnp.exp(car[:, :1]) * _dot_nt(w.astype(BF16), v_pages[p][...].astype(BF16))
            car_ref[...] = car - jnp.sum(sp, axis=1, keepdims=True)
    o_ref[0] = _collapse_heads(acc_ref[...], hmask_rows, go_ref[...], n_tok, n_heads).astype(o_ref.dtype)


def _sample_attn_call(body, name, page_table, q, k_new, v_new, g_out, cache_k, cache_v, group, scratch_shapes=()):
    db, n_pages = page_table.shape
    _, t, w = q.shape
    n_heads = w // HEAD_DIM
    tok = lambda: pl.BlockSpec((group, t, w), lambda bi, pt: (bi, 0, 0))

    def page(g, p):
        return pl.BlockSpec((None, w, PAGE_SIZE), lambda bi, pt: (pt[bi * group + g, p], 0, 0))

    pages = [page(g, p) for g in range(group) for p in range(n_pages)]
    grid_spec = pltpu.PrefetchScalarGridSpec(
        num_scalar_prefetch=1,
        grid=(db // group,),
        in_specs=[tok(), tok(), tok(), pl.BlockSpec((1, w), lambda bi, pt: (0, 0))] + pages * 2,
        out_specs=tok(),
        scratch_shapes=scratch_shapes,
    )
    return pl.pallas_call(
        functools.partial(body, n_pages=n_pages, n_tok=t, n_heads=n_heads, group=group),
        out_shape=jax.ShapeDtypeStruct(q.shape, BF16),
        grid_spec=grid_spec,
        compiler_params=_params(1),
        name=name,
    )(page_table, q, k_new, v_new, g_out, *([cache_k] * len(pages)), *([cache_v] * len(pages)))


def _post_kernel(x_ref, g0_ref, b0_ref, ma_ref, mb_ref, mod_ref, wo_ref, g1_ref, b1_ref,
                 wg_ref, wu_ref, wd_ref, g2_ref, b2_ref, o_ref, act_ref, *, alpha, wa, ff_chunk):
    xn = _layer_norm(x_ref[...], g0_ref[...], b0_ref[...])
    attn = _dot(ma_ref[...], wo_ref[:wa, :]) + _dot(mb_ref[...], wo_ref[wa:, :])
    x1 = _layer_norm(alpha * xn + mod_ref[2] * attn, g1_ref[...], b1_ref[...])
    hb = (x1 * (1.0 + mod_ref[4]) + mod_ref[3]).astype(BF16)
    for c in range(0, wg_ref.shape[1], ff_chunk):
        gate = _dot(hb, wg_ref[:, c:c + ff_chunk])
        up = _dot(hb, wu_ref[:, c:c + ff_chunk])
        act_ref[:, c:c + ff_chunk] = (gate / (1.0 + jnp.exp(-gate)) * up).astype(BF16)
    f = _dot(act_ref[...], wd_ref[...])
    o_ref[...] = _layer_norm(alpha * x1 + mod_ref[5] * f, g2_ref[...], b2_ref[...])


def _post_call(x2, g0, b0, ma, mb, mods, mod_spec, wo, g1, b1, wg, wu, wd, g2, b2, alpha, tm, ff_chunk):
    r, d = x2.shape
    wa = ma.shape[1]
    row = lambda w: pl.BlockSpec((tm, w), lambda i: (i, 0))
    vec = lambda: _const_spec((1, d))
    return pl.pallas_call(
        functools.partial(_post_kernel, alpha=alpha, wa=wa, ff_chunk=ff_chunk),
        out_shape=jax.ShapeDtypeStruct((r, d), F32),
        grid=(r // tm,),
        in_specs=[row(d), vec(), vec(), row(wa), row(mb.shape[1]), mod_spec, _const_spec(wo.shape), vec(), vec(),
                  _const_spec(wg.shape), _const_spec(wu.shape), _const_spec(wd.shape), vec(), vec()],
        out_specs=row(d),
        scratch_shapes=[pltpu.VMEM((tm, wg.shape[1]), BF16)],
        compiler_params=_params(1),
        name="post",
    )(x2, g0, b0, ma, mb, mods, wo, g1, b1, wg, wu, wd, g2, b2)


def _rotary_tables(pos):
    half = ROT_DIM // 2
    inv = ROPE_THETA ** (-jnp.arange(half, dtype=F32) * 2.0 / ROT_DIM)
    ang = pos.astype(F32)[:, None] * inv
    cos, sin = jnp.cos(ang), jnp.sin(ang)
    n = pos.shape[0]
    rest = HEAD_DIM - ROT_DIM
    c = jnp.concatenate([cos, cos, jnp.ones((n, rest), F32)], axis=1)
    s_up = jnp.concatenate([-sin, jnp.zeros((n, half + rest), F32)], axis=1)
    s_dn = jnp.concatenate([jnp.zeros((n, half), F32), sin, jnp.zeros((n, rest), F32)], axis=1)
    return jnp.stack([jnp.tile(t, (1, HEADS_PER_TILE)) for t in (c, s_up, s_dn)])


def kernel(x_prompt, x_sample, cache_moba_k, cache_moba_v, cache_sb_k, cache_sb_v, page_table, c_prompt, c_sample,
           ln0_g, ln0_b, w_mod, b_mod, w_in, g_out_a, g_out_b, w_out, ln1_g, ln1_b, w_gate, w_up, w_down,
           ln2_g, ln2_b):
    b, s, d = x_prompt.shape
    db, t, _ = x_sample.shape
    depth = w_mod.shape[0]
    assert depth == 1, "ln0 is fused into the first layer's kernels; only a one-layer trunk is supported"
    wa, wb = g_out_a.shape[1], g_out_b.shape[1]
    ha, hb = wa // HEAD_DIM, wb // HEAD_DIM
    n_pages = page_table.shape[1]
    past = n_pages * PAGE_SIZE
    assert s % MOBA_BLOCK == 0 and past % MOBA_BLOCK == 0
    alpha = (2 * depth) ** 0.25
    vec = lambda a: a.reshape(1, -1)

    n_mod = 6
    rows = b + db
    pad = -rows % 8
    c_all = jnp.concatenate([c_prompt, c_sample, jnp.zeros((pad, d), F32)], axis=0)
    m = _mod_call(c_all, w_mod[0], b_mod[0].reshape(1, -1))
    mods_p = m[:b].reshape(b, n_mod, 1, d)
    mods_s = jnp.repeat(m[b:rows].reshape(db, n_mod, d), t, axis=0).transpose(1, 0, 2)

    w_in_bf = w_in[0].astype(BF16)
    wo_bf, wg_bf, wu_bf, wd_bf = (w[0].astype(BF16) for w in (w_out, w_gate, w_up, w_down))
    g0, b0 = vec(ln0_g), vec(ln0_b)
    ln = [vec(a[0]) for a in (ln1_g, ln1_b, ln2_g, ln2_b)]
    goa, gob = g_out_a[0].reshape(1, wa), g_out_b[0].reshape(1, wb)

    tm = 512
    ff_chunk = 256

    xp2 = x_prompt.reshape(b * s, d)
    mod_spec_p = pl.BlockSpec((None, n_mod, 1, d), lambda i: (i // (s // tm), 0, 0, 0))
    rot_p = _rotary_tables(jnp.arange(s, dtype=jnp.int32))
    (qa, qb, ka, va, kb, vb, kab, vab, kbb, vbb, kmean) = _pre_call(
        xp2, g0, b0, mods_p, mod_spec_p, rot_p, w_in_bf, wa, wb, tm, True)
    mixed_a = _moba_call(qa.reshape(b, s, wa), kab, vab.reshape(b, s, wa), kmean.reshape(b, s // MOBA_BLOCK, wa),
                         goa, nb=1, n_tile=wa // LANES).reshape(b * s, wa)
    mixed_b = _sb_call(qb, kbb, vbb, gob, b, s, n_chain=4)
    y_p = _post_call(xp2, g0, b0, mixed_a, mixed_b, mods_p, mod_spec_p, wo_bf, ln[0], ln[1],
                     wg_bf, wu_bf, wd_bf, ln[2], ln[3], alpha, tm, ff_chunk)

    rs = db * t
    xs2 = x_sample.reshape(rs, d)
    mod_spec_s = pl.BlockSpec((n_mod, rs, d), lambda i: (0, 0, 0))
    rot_s = _rotary_tables(jnp.tile(past + jnp.arange(t, dtype=jnp.int32), db))
    (sqa, sqb, ska, sva, skb, svb) = _pre_call(
        xs2, g0, b0, mods_s, mod_spec_s, rot_s, w_in_bf, wa, wb, rs, False)
    tok3 = lambda a: a.reshape(db, t, a.shape[-1])
    pages = lambda c: c[0].transpose(0, 2, 3, 1).reshape(c.shape[1], -1, PAGE_SIZE)
    smixed_a = _sample_attn_call(_smoba_kernel, "smoba", page_table, tok3(sqa), tok3(ska), tok3(sva), goa,
                                 pages(cache_moba_k), pages(cache_moba_v), group=SMOBA_GROUP)
    sb_args = (page_table, tok3(sqb), tok3(skb), tok3(svb), gob, pages(cache_sb_k), pages(cache_sb_v))
    recent_b, log_keep = _ssb_recent_call(*sb_args, n_recent=min(SB_RECENT_PAGES, n_pages), group=SB_GROUP)
    smixed_b = lax.cond(
        jnp.max(log_keep) > UNDERFLOW_LOG,
        lambda: _sample_attn_call(_ssb_kernel, "ssb", *sb_args, group=1, scratch_shapes=[
            pltpu.VMEM((t * hb, wb), F32), pltpu.VMEM((t * hb, LANES), F32)]),
        lambda: recent_b)
    y_s = _post_call(xs2, g0, b0, smixed_a.reshape(rs, wa), smixed_b.reshape(rs, wb), mods_s, mod_spec_s,
                     wo_bf, ln[0], ln[1], wg_bf, wu_bf, wd_bf, ln[2], ln[3], alpha, rs, ff_chunk)

    kv_p = lambda a, h: a.reshape(b, h, HEAD_DIM, s).transpose(0, 3, 1, 2)[None]
    return (y_p.reshape(b, s, d), y_s.reshape(db, t, d),
            kv_p(ka, ha), kv_p(va, ha), kv_p(kb, hb), kv_p(vb, hb),
            ska.reshape(1, db, t, ha, HEAD_DIM), sva.reshape(1, db, t, ha, HEAD_DIM),
            skb.reshape(1, db, t, hb, HEAD_DIM), svb.reshape(1, db, t, hb, HEAD_DIM))
```
